```python
import math
import jax, jax.numpy as jnp
from jax import lax
import numpy as np

D_MODEL = 1024
BATCH = 8
SEQ = 2048
DEPTH = 4

MIX_WIDTH = D_MODEL
N_MIXERS = 4
GROUP_WIDTH = MIX_WIDTH // N_MIXERS
ATT_HEADS = 4
ATT_V_DIM = GROUP_WIDTH // ATT_HEADS
ATT_QK_DIM = ATT_V_DIM // 2
Q_BLOCK = 128
NUM_BUCKETS = 32
MAX_DISTANCE = 128
CONF_WIDTH = 31
SCONV_WIDTH = 3
SGU_HEADS = 4
SGU_HEAD_DIM = GROUP_WIDTH // SGU_HEADS
SGU_CHUNK = 128
PEER_HEADS = 8
PEER_NKEYS = 128
PEER_EXPERTS = PEER_NKEYS * PEER_NKEYS
PEER_TOPK = 16
PEER_QDIM = 256
PEER_HALF = PEER_QDIM // 2
PEER_TOK_BLOCK = 128
N_IN_SLICES = 10
MIX_IN = N_IN_SLICES * GROUP_WIDTH
N_MOD = 6
EPS = 1e-6
NEG_INF = -1e30

kernel_name = 'hybrid_diffattn_conv_sgu_peer'


def rms_norm(x, g):
    xf = x.astype(jnp.float32)
    y = xf * lax.rsqrt(jnp.mean(xf * xf, axis=-1, keepdims=True) + EPS)
    return (y * g.astype(jnp.float32)).astype(x.dtype)


def layer_norm(x, g, b):
    xf = x.astype(jnp.float32)
    mu = jnp.mean(xf, axis=-1, keepdims=True)
    var = jnp.mean(jnp.square(xf - mu), axis=-1, keepdims=True)
    y = (xf - mu) * lax.rsqrt(var + EPS)
    return (y * g.astype(jnp.float32) + b.astype(jnp.float32)).astype(x.dtype)


def causal_depthwise_conv(x, w):
    k, ch = w.shape
    return lax.conv_general_dilated(
        x, w[:, None, :].astype(x.dtype), window_strides=(1,), padding=[(k - 1, 0)],
        dimension_numbers=('NWC', 'WIO', 'NWC'), feature_group_count=ch)


def rel_bucket(rel):
    n = jnp.maximum(rel, 0)
    max_exact = NUM_BUCKETS // 2
    ratio = jnp.log(jnp.maximum(n, 1).astype(jnp.float32) / max_exact) / math.log(MAX_DISTANCE / max_exact)
    large = jnp.minimum(max_exact + (ratio * (NUM_BUCKETS - max_exact)).astype(jnp.int32), NUM_BUCKETS - 1)
    return jnp.where(n < max_exact, n, large)


def diff_attention(q, k, v, rel_bias, lam, subln_g, lam_init):
    b, s = q.shape[:2]
    scale = ATT_QK_DIM ** -0.5
    outs = []
    for i in range(s // Q_BLOCK):
        q0 = i * Q_BLOCK
        end = q0 + Q_BLOCK
        logits = jnp.einsum('bqhmd,bkhmd->bhmqk', q[:, q0:end], k[:, :end]).astype(jnp.float32) * scale
        rel = (q0 + jnp.arange(Q_BLOCK))[:, None] - jnp.arange(end)[None, :]
        bias = rel_bias[rel_bucket(rel)].astype(jnp.float32)
        logits = logits + jnp.transpose(bias, (2, 0, 1))[None, :, None]
        logits = jnp.where(rel >= 0, logits, NEG_INF)
        p = jax.nn.softmax(logits, axis=-1)
        w = (p[:, :, 0] - lam * p[:, :, 1]).astype(v.dtype)
        outs.append(jnp.einsum('bhqk,bkhd->bqhd', w, v[:, :end]))
    o = jnp.concatenate(outs, axis=1)
    o = rms_norm(o, subln_g) * (1.0 - lam_init)
    return o.reshape(b, s, ATT_HEADS * ATT_V_DIM)


def spatial_gating(u, v, ln_g, ln_b, w_s, b_s):
    b, s, _ = v.shape
    v = layer_norm(v, ln_g, ln_b).reshape(b, s // SGU_CHUNK, SGU_CHUNK, SGU_HEADS, SGU_HEAD_DIM)
    w = jnp.tril(w_s)
    z = jnp.einsum('hts,bcshd->bcthd', w, v) + jnp.transpose(b_s)[:, :, None]
    return u * z.reshape(b, s, GROUP_WIDTH)


def hybrid_mixer(h, w_in, w_out, lam_par, subln_g, conf_dw, conf_ln_g, conf_ln_b, sconv_w,
                 sgu_ln_g, sgu_ln_b, sgu_w, sgu_b, rel_bias, lam_init):
    b, s, _ = h.shape
    q, k, v, ga, gb, cb, cc, ch, su, sv = jnp.split(h @ w_in, N_IN_SLICES, axis=-1)
    lp = lam_par.astype(jnp.float32)
    lam = jnp.exp(jnp.sum(lp[0] * lp[1])) - jnp.exp(jnp.sum(lp[2] * lp[3])) + lam_init
    y_a = diff_attention(q.reshape(b, s, ATT_HEADS, 2, ATT_QK_DIM),
                         k.reshape(b, s, ATT_HEADS, 2, ATT_QK_DIM),
                         v.reshape(b, s, ATT_HEADS, ATT_V_DIM), rel_bias, lam, subln_g, lam_init)
    z = causal_depthwise_conv(ga * jax.nn.sigmoid(gb), conf_dw)
    y_b = jax.nn.silu(layer_norm(z, conf_ln_g, conf_ln_b))
    y_c = cb * causal_depthwise_conv(cc * ch, sconv_w)
    y_d = spatial_gating(jax.nn.gelu(su, approximate=False), jax.nn.gelu(sv, approximate=False),
                         sgu_ln_g, sgu_ln_b, sgu_w, sgu_b)
    return jnp.concatenate([y_a, y_b, y_c, y_d], axis=-1) @ w_out


def peer_ffn(h, wq, keys, u_tab, v_tab):
    b, s, d = h.shape
    q = (h @ wq).reshape(b, s, PEER_HEADS, 2, PEER_HALF)
    sc = jnp.einsum('bshpd,hpnd->bshpn', q, keys).astype(jnp.float32)
    half_s, half_i = lax.top_k(sc, PEER_TOPK)
    cand = (half_s[..., 0, :, None] + half_s[..., 1, None, :]).reshape(b, s, PEER_HEADS, PEER_TOPK * PEER_TOPK)
    top_s, top_c = lax.top_k(cand, PEER_TOPK)
    i1 = jnp.take_along_axis(half_i[..., 0, :], top_c // PEER_TOPK, axis=-1)
    i2 = jnp.take_along_axis(half_i[..., 1, :], top_c % PEER_TOPK, axis=-1)
    expert = i1 * PEER_NKEYS + i2
    gate = jax.nn.softmax(top_s, axis=-1).astype(h.dtype)
    n_blk = (b * s) // PEER_TOK_BLOCK

    def block(args):
        hb, eb, gb = args
        act = jax.nn.gelu(jnp.einsum('td,thkd->thk', hb, u_tab[eb]), approximate=False)
        return jnp.einsum('thk,thkd->td', gb * act, v_tab[eb])

    y = lax.map(block, (h.reshape(n_blk, PEER_TOK_BLOCK, d),
                        expert.reshape(n_blk, PEER_TOK_BLOCK, PEER_HEADS, PEER_TOPK),
                        gate.reshape(n_blk, PEER_TOK_BLOCK, PEER_HEADS, PEER_TOPK)))
    return y.reshape(b, s, d)


def setup_inputs(seed: int = 0) -> dict:
    key = jax.random.key(seed)
    ks = jax.random.split(key, 26)
    f32 = jnp.float32

    def nrm(k, shape, std):
        return jax.random.normal(k, shape, f32) * std

    L, D, G = DEPTH, D_MODEL, GROUP_WIDTH
    return {
        'x': nrm(ks[0], (BATCH, SEQ, D), 1.0),
        'c': nrm(ks[1], (BATCH, D), 1.0),
        'rel_bias': nrm(ks[2], (NUM_BUCKETS, ATT_HEADS), 0.5),
        'w_mod': nrm(ks[3], (L, D, N_MOD * D), 0.5 * D ** -0.5),
        'b_mod': nrm(ks[4], (L, N_MOD * D), 0.01),
        'norm1_g': 1.0 + nrm(ks[5], (L, D), 0.02),
        'norm2_g': 1.0 + nrm(ks[6], (L, D), 0.02),
        'w_in': nrm(ks[7], (L, D, MIX_IN), D ** -0.5),
        'w_out': nrm(ks[8], (L, MIX_WIDTH, D), MIX_WIDTH ** -0.5),
        'diff_lambda': nrm(ks[9], (L, 4, ATT_QK_DIM), 0.1),
        'subln_g': 1.0 + nrm(ks[10], (L, ATT_V_DIM), 0.02),
        'conf_dw': nrm(ks[11], (L, CONF_WIDTH, G), CONF_WIDTH ** -0.5),
        'conf_ln_g': 1.0 + nrm(ks[12], (L, G), 0.02),
        'conf_ln_b': nrm(ks[13], (L, G), 0.01),
        'sconv_w': nrm(ks[14], (L, SCONV_WIDTH, G), SCONV_WIDTH ** -0.5),
        'sgu_ln_g': 1.0 + nrm(ks[15], (L, G), 0.02),
        'sgu_ln_b': nrm(ks[16], (L, G), 0.01),
        'sgu_w': nrm(ks[17], (L, SGU_HEADS, SGU_CHUNK, SGU_CHUNK), SGU_CHUNK ** -0.5),
        'sgu_b': 1.0 + nrm(ks[18], (L, SGU_HEADS, SGU_CHUNK), 0.01),
        'peer_wq': nrm(ks[19], (L, D, PEER_HEADS * PEER_QDIM), D ** -0.5),
        'peer_keys': nrm(ks[20], (L, PEER_HEADS, 2, PEER_NKEYS, PEER_HALF), PEER_HALF ** -0.5),
        'peer_u': nrm(ks[21], (L, PEER_EXPERTS, D), D ** -0.5),
        'peer_v': nrm(ks[22], (L, PEER_EXPERTS, D), PEER_HEADS ** -0.5),
        'final_g': 1.0 + nrm(ks[23], (D,), 0.02),
    }


def reference(x, c, rel_bias, w_mod, b_mod, norm1_g, norm2_g, w_in, w_out, diff_lambda, subln_g,
              conf_dw, conf_ln_g, conf_ln_b, sconv_w, sgu_ln_g, sgu_ln_b, sgu_w, sgu_b,
              peer_wq, peer_keys, peer_u, peer_v, final_g):
    cond = jax.nn.silu(c)
    for l in range(DEPTH):
        lam_init = 0.8 - 0.6 * math.exp(-0.3 * l)
        mod = cond @ w_mod[l] + b_mod[l]
        sh1, sc1, g1, sh2, sc2, g2 = [m[:, None, :] for m in jnp.split(mod, N_MOD, axis=-1)]
        h = rms_norm(x, norm1_g[l]) * (1.0 + sc1) + sh1
        x = x + g1 * hybrid_mixer(h, w_in[l], w_out[l], diff_lambda[l], subln_g[l], conf_dw[l],
                                  conf_ln_g[l], conf_ln_b[l], sconv_w[l], sgu_ln_g[l], sgu_ln_b[l],
                                  sgu_w[l], sgu_b[l], rel_bias, lam_init)
        h = rms_norm(x, norm2_g[l]) * (1.0 + sc2) + sh2
        x = x + g2 * peer_ffn(h, peer_wq[l], peer_keys[l], peer_u[l], peer_v[l])
    return rms_norm(x, final_g)
```

```python
import functools
import math

import jax
import jax.numpy as jnp
import numpy as np
from jax import lax
from jax.experimental import pallas as pl
from jax.experimental.pallas import tpu as pltpu

F32 = jnp.float32
BF16 = jnp.bfloat16
I32 = jnp.int32
HIGHEST = lax.Precision.HIGHEST

GROUP = 256
N_SLICES = 10
ATT_HEADS = 4
ATT_V = 64
ATT_QK = 32
QB = 128
NUM_BUCKETS = 32
MAX_DISTANCE = 128
CONF_W = 31
SCONV_W = 3
SGU_HEADS = 4
SGU_CHUNK = 128
PEER_HEADS = 8
NKEYS = 128
TOPK = 16
N_MOD = 6
EPS = 1e-6
NEG_INF = -1e30
SQRT_HALF = float(np.sqrt(0.5))

LANES = 128
SUBLANES = 8
VMEM_LIMIT_BYTES = 56 * 1024 * 1024

CAND_COUNTS = [TOPK // (a + 1) for a in range(TOPK)]
CAND_STARTS = [int(v) for v in np.cumsum([0] + CAND_COUNTS[:-1])]
N_CAND = sum(CAND_COUNTS)
N_CAND_PAD = -(-N_CAND // SUBLANES) * SUBLANES


def _gelu(x):
    return 0.5 * x * (1.0 + lax.erf(x * SQRT_HALF))


def _cparams(*sem):
    return pltpu.CompilerParams(dimension_semantics=sem, vmem_limit_bytes=VMEM_LIMIT_BYTES)


def _mod_kernel(c_ref, w_ref, b_ref, o_ref):
    c = c_ref[...]
    cond = c * jax.nn.sigmoid(c)
    o_ref[0] = jnp.dot(cond, w_ref[0], precision=HIGHEST, preferred_element_type=F32) + b_ref[0]


def _mod_call(c, w_mod, b_mod):
    depth, d, n = w_mod.shape
    b = c.shape[0]
    tn = 2048
    return pl.pallas_call(
        _mod_kernel,
        grid=(depth, n // tn),
        in_specs=[pl.BlockSpec((b, d), lambda l, j: (0, 0)),
                  pl.BlockSpec((1, d, tn), lambda l, j: (l, 0, j)),
                  pl.BlockSpec((1, 1, tn), lambda l, j: (l, 0, j))],
        out_specs=pl.BlockSpec((1, b, tn), lambda l, j: (l, 0, j)),
        out_shape=jax.ShapeDtypeStruct((depth, b, n), F32),
        compiler_params=_cparams("arbitrary", "arbitrary"),
        name="mod",
    )(c, w_mod, b_mod.reshape(depth, 1, n))


def _modulated_norm(x, g, shift, scale):
    ms = jnp.mean(x * x, axis=-1, keepdims=True)
    return (x * lax.rsqrt(ms + EPS) * g) * (1.0 + scale) + shift


def _in_kernel(x_ref, mod_ref, g_ref, w_ref, qkv_ref, pc_ref):
    d = x_ref.shape[1]
    h = _modulated_norm(x_ref[...], g_ref[...], mod_ref[0, :, 0:d], mod_ref[0, :, d:2 * d])
    res = jnp.dot(h.astype(BF16), w_ref[...], preferred_element_type=F32)
    qkv_ref[:, 0:GROUP] = (res[:, 0:GROUP] * (ATT_QK ** -0.5)).astype(BF16)
    qkv_ref[:, GROUP:3 * GROUP] = res[:, GROUP:3 * GROUP].astype(BF16)
    pc_ref[...] = res[:, 3 * GROUP:]


def _in_call(x, mod, g, w_in, seq):
    t, d = x.shape
    n = w_in.shape[1]
    tm = 512
    return pl.pallas_call(
        _in_kernel,
        grid=(t // tm,),
        in_specs=[pl.BlockSpec((tm, d), lambda i: (i, 0)),
                  pl.BlockSpec((1, 1, N_MOD * d), lambda i: ((i * tm) // seq, 0, 0)),
                  pl.BlockSpec((1, d), lambda i: (0, 0)),
                  pl.BlockSpec((d, n), lambda i: (0, 0))],
        out_specs=[pl.BlockSpec((tm, 3 * GROUP), lambda i: (i, 0)),
                   pl.BlockSpec((tm, n - 3 * GROUP), lambda i: (i, 0))],
        out_shape=[jax.ShapeDtypeStruct((t, 3 * GROUP), BF16),
                   jax.ShapeDtypeStruct((t, n - 3 * GROUP), F32)],
        compiler_params=_cparams("arbitrary"),
        name="in_proj",
    )(x, mod, g, w_in)


def _attn_kernel(q_ref, k_ref, v_ref, bd_ref, bo_ref, bf_ref, lam_ref, g_ref, o_ref, *, lam_init):
    i = pl.program_id(1)
    q = q_ref[...]
    lane = lax.broadcasted_iota(I32, (1, GROUP), 1)
    lp = lam_ref[...]
    lam = (jnp.exp(jnp.sum(lp[0:1] * lp[1:2], axis=1, keepdims=True))
           - jnp.exp(jnp.sum(lp[2:3] * lp[3:4], axis=1, keepdims=True)) + lam_init)

    def scores(qm, j, bias):
        kb = k_ref[pl.ds(pl.multiple_of(j * QB, QB), QB), :]
        s = lax.dot_general(qm, kb, (((1,), (1,)), ((), ())), preferred_element_type=F32)
        return s + bias

    def update(carry, s, j):
        mx, l, acc = carry
        mn = jnp.maximum(mx, jnp.max(s, axis=1, keepdims=True))
        alpha = jnp.exp(mx - mn)
        pm = jnp.exp(s - mn)
        vb = v_ref[pl.ds(pl.multiple_of(j * QB, QB), QB), :]
        l = alpha * l + jnp.sum(pm, axis=1, keepdims=True)
        acc = alpha * acc + jnp.dot(pm.astype(BF16), vb, preferred_element_type=F32)
        return mn, l, acc

    out = jnp.zeros((QB, GROUP), F32)
    for h in range(ATT_HEADS):
        maps = []
        for m in range(2):
            lo = h * ATT_V + m * ATT_QK
            qm = jnp.where((lane >= lo) & (lane < lo + ATT_QK), q, jnp.zeros_like(q))
            bias_far = bf_ref[h]
            s = scores(qm, i, bd_ref[h])
            mx = jnp.max(s, axis=1, keepdims=True)
            pm = jnp.exp(s - mx)
            vb = v_ref[pl.ds(pl.multiple_of(i * QB, QB), QB), :]
            carry = (mx, jnp.sum(pm, axis=1, keepdims=True),
                     jnp.dot(pm.astype(BF16), vb, preferred_element_type=F32))
            carry = lax.cond(
                i >= 1,
                lambda c, qm=qm, h=h: update(c, scores(qm, i - 1, bo_ref[h]), i - 1),
                lambda c: c,
                carry)
            carry = lax.fori_loop(
                0, jnp.maximum(i - 1, 0),
                lambda j, c, qm=qm, bias_far=bias_far: update(c, scores(qm, j, bias_far), j),
                carry)
            maps.append(carry[2] / carry[1])
        o_h = maps[0] - lam * maps[1]
        out = jnp.where((lane >= h * ATT_V) & (lane < (h + 1) * ATT_V), o_h, out)

    r = lax.broadcasted_iota(I32, (GROUP, GROUP), 0) // ATT_V
    cidx = lax.broadcasted_iota(I32, (GROUP, GROUP), 1) // ATT_V
    avg = jnp.where(r == cidx, 1.0 / ATT_V, 0.0).astype(F32)
    ms = jnp.dot(out * out, avg, precision=HIGHEST, preferred_element_type=F32)
    y = out * lax.rsqrt(ms + EPS) * g_ref[...] * (1.0 - lam_init)
    o_ref[...] = y.astype(o_ref.dtype)


def _attn_call(qkv, bias_diag, bias_off, bias_far, lam_par, subln_g4, batch, seq, lam_init):
    t = qkv.shape[0]
    nq = seq // QB
    return pl.pallas_call(
        functools.partial(_attn_kernel, lam_init=lam_init),
        grid=(batch, nq),
        in_specs=[pl.BlockSpec((QB, GROUP), lambda b, i: (b * nq + i, 0)),
                  pl.BlockSpec((seq, GROUP), lambda b, i: (b, 1)),
                  pl.BlockSpec((seq, GROUP), lambda b, i: (b, 2)),
                  pl.BlockSpec((ATT_HEADS, QB, QB), lambda b, i: (0, 0, 0)),
                  pl.BlockSpec((ATT_HEADS, QB, QB), lambda b, i: (0, 0, 0)),
                  pl.BlockSpec((ATT_HEADS, 1, QB), lambda b, i: (0, 0, 0)),
                  pl.BlockSpec((4, ATT_QK), lambda b, i: (0, 0)),
                  pl.BlockSpec((1, GROUP), lambda b, i: (0, 0))],
        out_specs=pl.BlockSpec((QB, GROUP), lambda b, i: (b * nq + i, 0)),
        out_shape=jax.ShapeDtypeStruct((t, GROUP), BF16),
        compiler_params=_cparams("arbitrary", "arbitrary"),
        name="diff_attn",
    )(qkv, qkv, qkv, bias_diag, bias_off, bias_far, lam_par, subln_g4)


def _rel_bucket_np(n):
    max_exact = NUM_BUCKETS // 2
    n = np.maximum(n, 0)
    ratio = (np.log(np.maximum(n, 1).astype(np.float32) / np.float32(max_exact))
             / np.float32(math.log(MAX_DISTANCE / max_exact)))
    large = np.minimum(max_exact + (ratio * (NUM_BUCKETS - max_exact)).astype(np.int32), NUM_BUCKETS - 1)
    return np.where(n < max_exact, n, large).astype(np.int32)


def _attn_bias_tiles(rel_bias):
    dq = np.arange(QB)[:, None] - np.arange(QB)[None, :]
    diag = jnp.transpose(rel_bias[_rel_bucket_np(dq)], (2, 0, 1))
    diag = jnp.where(jnp.asarray(dq >= 0)[None], diag, NEG_INF)
    off = jnp.transpose(rel_bias[_rel_bucket_np(dq + QB)], (2, 0, 1))
    assert np.all(_rel_bucket_np(np.arange(QB + 1, 4 * QB)) == NUM_BUCKETS - 1)
    far = jnp.broadcast_to(rel_bias[NUM_BUCKETS - 1][:, None, None], (ATT_HEADS, 1, QB))
    return diag.astype(F32), off.astype(F32), far.astype(F32)


CONV_HALO = 32
SCONV_HALO = 8


def _layer_norm(x, g, b):
    mu = jnp.mean(x, axis=-1, keepdims=True)
    xc = x - mu
    var = jnp.mean(xc * xc, axis=-1, keepdims=True)
    return xc * lax.rsqrt(var + EPS) * g + b


def _conv_kernel(cur_ref, hga_ref, hgb_ref, hcc_ref, hch_ref, dw_ref, lng_ref, lnb_ref, sw_ref,
                 sg_ref, sb_ref, ws_ref, bs_ref, o_ref, ext_ref, ext2_ref):
    i = pl.program_id(1)
    ts = cur_ref.shape[0]
    first = i == 0

    def col(k):
        return cur_ref[:, k * GROUP:(k + 1) * GROUP]

    zh = hga_ref[...] * jax.nn.sigmoid(hgb_ref[...])
    ext_ref[0:CONV_HALO, :] = jnp.where(first, 0.0, zh)
    ext_ref[CONV_HALO:, :] = col(0) * jax.nn.sigmoid(col(1))
    acc = jnp.zeros((ts, GROUP), F32)
    for j in range(CONF_W):
        off = CONV_HALO - (CONF_W - 1) + j
        acc = acc + dw_ref[j:j + 1, :] * ext_ref[off:off + ts, :]
    zb = _layer_norm(acc, lng_ref[...], lnb_ref[...])
    o_ref[:, 0:GROUP] = (zb * jax.nn.sigmoid(zb)).astype(o_ref.dtype)

    ext2_ref[0:SCONV_HALO, :] = jnp.where(first, 0.0, hcc_ref[...] * hch_ref[...])
    ext2_ref[SCONV_HALO:, :] = col(3) * col(4)
    acc = jnp.zeros((ts, GROUP), F32)
    for j in range(SCONV_W):
        off = SCONV_HALO - (SCONV_W - 1) + j
        acc = acc + sw_ref[j:j + 1, :] * ext2_ref[off:off + ts, :]
    o_ref[:, GROUP:2 * GROUP] = (col(2) * acc).astype(o_ref.dtype)

    lane = lax.broadcasted_iota(I32, (1, GROUP), 1)
    rr = lax.broadcasted_iota(I32, (SGU_CHUNK, SGU_CHUNK), 0)
    cc = lax.broadcasted_iota(I32, (SGU_CHUNK, SGU_CHUNK), 1)
    hd = GROUP // SGU_HEADS
    w_tril = [jnp.where(rr >= cc, ws_ref[h], 0.0).astype(BF16) for h in range(SGU_HEADS)]
    for c in range(ts // SGU_CHUNK):
        rows = slice(c * SGU_CHUNK, (c + 1) * SGU_CHUNK)
        u = _gelu(cur_ref[rows, 5 * GROUP:6 * GROUP])
        v = _layer_norm(_gelu(cur_ref[rows, 6 * GROUP:7 * GROUP]), sg_ref[...], sb_ref[...]).astype(BF16)
        z = bs_ref[...]
        for h in range(SGU_HEADS):
            zh_ = jnp.dot(w_tril[h], v, preferred_element_type=F32)
            z = z + jnp.where((lane >= h * hd) & (lane < (h + 1) * hd), zh_, 0.0)
        o_ref[rows, 2 * GROUP:3 * GROUP] = (u * z).astype(o_ref.dtype)


def _conv_call(pc, conf_dw, conf_ln_g, conf_ln_b, sconv_w, sgu_ln_g, sgu_ln_b, sgu_w, sgu_bias_tile,
               batch, seq):
    t, n = pc.shape
    ts = 512
    nst = seq // ts

    def halo_map(rows, colblk):
        per = ts // rows
        return lambda b, i: (jnp.maximum((b * nst + i) * per - 1, 0), colblk)

    vec = lambda: pl.BlockSpec((1, GROUP), lambda b, i: (0, 0))
    return pl.pallas_call(
        _conv_kernel,
        grid=(batch, nst),
        in_specs=[pl.BlockSpec((ts, n), lambda b, i: (b * nst + i, 0)),
                  pl.BlockSpec((CONV_HALO, GROUP), halo_map(CONV_HALO, 0)),
                  pl.BlockSpec((CONV_HALO, GROUP), halo_map(CONV_HALO, 1)),
                  pl.BlockSpec((SCONV_HALO, GROUP), halo_map(SCONV_HALO, 3)),
                  pl.BlockSpec((SCONV_HALO, GROUP), halo_map(SCONV_HALO, 4)),
                  pl.BlockSpec((CONF_W, GROUP), lambda b, i: (0, 0)),
                  vec(), vec(),
                  pl.BlockSpec((SCONV_W, GROUP), lambda b, i: (0, 0)),
                  vec(), vec(),
                  pl.BlockSpec((SGU_HEADS, SGU_CHUNK, SGU_CHUNK), lambda b, i: (0, 0, 0)),
                  pl.BlockSpec((SGU_CHUNK, GROUP), lambda b, i: (0, 0))],
        out_specs=pl.BlockSpec((ts, 3 * GROUP), lambda b, i: (b * nst + i, 0)),
        out_shape=jax.ShapeDtypeStruct((t, 3 * GROUP), BF16),
        scratch_shapes=[pltpu.VMEM((CONV_HALO + ts, GROUP), F32),
                        pltpu.VMEM((SCONV_HALO + ts, GROUP), F32)],
        compiler_params=_cparams("arbitrary", "arbitrary"),
        name="conv_mix",
    )(pc, pc, pc, pc, pc, conf_dw, conf_ln_g, conf_ln_b, sconv_w, sgu_ln_g, sgu_ln_b, sgu_w,
      sgu_bias_tile)


def _extract_top(s, rowi, n_rounds, want_rank):
    sentinel = float(s.shape[0])
    vals = []
    mark = jnp.full(s.shape, float(n_rounds) if want_rank else 0.0, F32)
    for r in range(n_rounds):
        m = jnp.max(s, axis=0, keepdims=True)
        idx = jnp.min(jnp.where(s == m, rowi, sentinel), axis=0, keepdims=True)
        hit = rowi == idx
        mark = jnp.where(hit, float(r) if want_rank else 1.0, mark)
        s = jnp.where(hit, -jnp.inf, s)
        vals.append(m)
    return vals, mark


def _outtop_kernel(x_ref, ya_ref, yb_ref, mod_ref, wo_ref, g_ref, wq_ref, keys_ref,
                   xo_ref, h2_ref, c1_ref, e1_ref, r2_ref, e2_ref, qs_ref):
    d = x_ref.shape[1]
    tm = x_ref.shape[0]
    y = jnp.dot(ya_ref[...], wo_ref[0:GROUP, :], preferred_element_type=F32)
    y = y + jnp.dot(yb_ref[...], wo_ref[GROUP:, :], preferred_element_type=F32)
    x = x_ref[...] + mod_ref[0, :, 2 * d:3 * d] * y
    xo_ref[...] = x
    h2 = _modulated_norm(x, g_ref[...], mod_ref[0, :, 3 * d:4 * d], mod_ref[0, :, 4 * d:5 * d]).astype(BF16)
    h2_ref[...] = h2
    q = jnp.dot(h2, wq_ref[...], preferred_element_type=F32).astype(BF16)
    for hp in range(2 * PEER_HEADS):
        qs_ref[hp] = q[:, hp * NKEYS:(hp + 1) * NKEYS]

    rowi = lax.broadcasted_iota(I32, (NKEYS, tm), 0).astype(F32)
    rowc = lax.broadcasted_iota(I32, (N_CAND_PAD, tm), 0)
    rowc_f = rowc.astype(F32)
    a_row = jnp.zeros((N_CAND_PAD, tm), I32) - 1
    for st in CAND_STARTS:
        a_row = a_row + (rowc >= st).astype(I32)
    a_row = jnp.where(rowc >= N_CAND, TOPK, a_row)
    start_row = jnp.zeros((N_CAND_PAD, tm), I32)
    for a, st in enumerate(CAND_STARTS):
        start_row = jnp.where(a_row == a, st, start_row)
    b_row = rowc - start_row

    def head(hh, carry):
        halves = []
        for p in range(2):
            s = lax.dot_general(keys_ref[2 * hh + p], qs_ref[2 * hh + p], (((1,), (1,)), ((), ())),
                                preferred_element_type=F32)
            vals, rank = _extract_top(s, rowi, TOPK, True)
            halves.append((s, vals, rank))
        (s1, v1, r1), (s2, v2, r2) = halves
        rep1 = jnp.full((N_CAND_PAD, tm), -jnp.inf, F32)
        rep2 = jnp.full((N_CAND_PAD, tm), -jnp.inf, F32)
        for a in range(TOPK):
            rep1 = jnp.where(a_row == a, v1[a], rep1)
            rep2 = jnp.where(b_row == a, v2[a], rep2)
        cand = rep1 + rep2
        _, sel = _extract_top(cand, rowc_f, TOPK, False)
        cmax = v1[0] + v2[0]
        z = jnp.sum(sel * jnp.exp(jnp.where(sel > 0.0, cand, cmax) - cmax), axis=0, keepdims=True)
        c1 = jnp.zeros((NKEYS, tm), F32)
        for a in range(TOPK):
            cnt = jnp.sum(jnp.where(a_row == a, sel, 0.0), axis=0, keepdims=True)
            c1 = jnp.where(r1 == float(a), cnt, c1)
        c1_ref[hh] = c1
        e1_ref[hh] = jnp.exp(s1 - v1[0]) / z
        r2_ref[hh] = r2
        e2_ref[hh] = jnp.exp(s2 - v2[0])
        return carry

    lax.fori_loop(0, PEER_HEADS, head, 0)


def _outtop_call(x, ya, ybcd, mod, w_out, g2, wq, keys, seq):
    t, d = x.shape
    tm = 256
    nq = wq.shape[1]
    tab = lambda: pl.BlockSpec((PEER_HEADS, NKEYS, tm), lambda i: (0, 0, i))
    tab_shape = jax.ShapeDtypeStruct((PEER_HEADS, NKEYS, t), F32)
    return pl.pallas_call(
        _outtop_kernel,
        grid=(t // tm,),
        in_specs=[pl.BlockSpec((tm, d), lambda i: (i, 0)),
                  pl.BlockSpec((tm, GROUP), lambda i: (i, 0)),
                  pl.BlockSpec((tm, 3 * GROUP), lambda i: (i, 0)),
                  pl.BlockSpec((1, 1, N_MOD * d), lambda i: ((i * tm) // seq, 0, 0)),
                  pl.BlockSpec((4 * GROUP, d), lambda i: (0, 0)),
                  pl.BlockSpec((1, d), lambda i: (0, 0)),
                  pl.BlockSpec((d, nq), lambda i: (0, 0)),
                  pl.BlockSpec((2 * PEER_HEADS, NKEYS, NKEYS), lambda i: (0, 0, 0))],
        out_specs=[pl.BlockSpec((tm, d), lambda i: (i, 0)),
                   pl.BlockSpec((tm, d), lambda i: (i, 0)),
                   tab(), tab(), tab(), tab()],
        out_shape=[jax.ShapeDtypeStruct((t, d), F32),
                   jax.ShapeDtypeStruct((t, d), BF16),
                   tab_shape, tab_shape, tab_shape, tab_shape],
        scratch_shapes=[pltpu.VMEM((2 * PEER_HEADS, tm, NKEYS), BF16)],
        compiler_params=_cparams("arbitrary"),
        name="out_topk",
    )(x, ya, ybcd, mod, w_out, g2, wq, keys)


def _peer_kernel(x_ref, h2_ref, mod_ref, c1_ref, e1_ref, r2_ref, e2_ref, u_ref, vt_ref, fg_ref,
                 o_ref, acc_ref, a_ref, w_ref, *, final):
    k = pl.program_id(1)
    nb = u_ref.shape[0] // NKEYS
    d = x_ref.shape[1]

    @pl.when(k == 0)
    def _():
        acc_ref[...] = jnp.zeros_like(acc_ref)

    a_ref[...] = lax.dot_general(u_ref[...], h2_ref[...], (((1,), (1,)), ((), ())),
                                 preferred_element_type=F32)

    def block(ii, carry):
        rows = pl.ds(pl.multiple_of(ii * NKEYS, NKEYS), NKEYS)
        i1 = k * nb + ii
        act = _gelu(a_ref[rows, :])
        gate = jnp.zeros_like(act)
        for hh in range(PEER_HEADS):
            c1 = c1_ref[hh, pl.ds(i1, 1), :]
            e1 = e1_ref[hh, pl.ds(i1, 1), :]
            gate = gate + jnp.where(r2_ref[hh] < c1, e2_ref[hh] * e1, 0.0)
        w_ref[rows, :] = (gate * act).astype(BF16)
        return carry

    lax.fori_loop(0, nb, block, 0)
    acc_ref[...] += jnp.dot(vt_ref[...], w_ref[...], preferred_element_type=F32)

    @pl.when(k == pl.num_programs(1) - 1)
    def _():
        y = jnp.transpose(acc_ref[...])
        x = x_ref[...] + mod_ref[0, :, 5 * d:6 * d] * y
        if final:
            ms = jnp.mean(x * x, axis=-1, keepdims=True)
            x = x * lax.rsqrt(ms + EPS) * fg_ref[...]
        o_ref[...] = x


def _peer_call(x, h2, mod, c1, e1, r2, e2, u, vt, final_g, seq, final):
    t, d = x.shape
    ne = u.shape[0]
    tm = 512
    ec = 1024
    tab = lambda: pl.BlockSpec((PEER_HEADS, NKEYS, tm), lambda i, k: (0, 0, i))
    return pl.pallas_call(
        functools.partial(_peer_kernel, final=final),
        grid=(t // tm, ne // ec),
        in_specs=[pl.BlockSpec((tm, d), lambda i, k: (i, 0)),
                  pl.BlockSpec((tm, d), lambda i, k: (i, 0)),
                  pl.BlockSpec((1, 1, N_MOD * d), lambda i, k: ((i * tm) // seq, 0, 0)),
                  tab(), tab(), tab(), tab(),
                  pl.BlockSpec((ec, d), lambda i, k: (k, 0)),
                  pl.BlockSpec((d, ec), lambda i, k: (0, k)),
                  pl.BlockSpec((1, d), lambda i, k: (0, 0))],
        out_specs=pl.BlockSpec((tm, d), lambda i, k: (i, 0)),
        out_shape=jax.ShapeDtypeStruct((t, d), F32),
        scratch_shapes=[pltpu.VMEM((d, tm), F32),
                        pltpu.VMEM((ec, tm), F32),
                        pltpu.VMEM((ec, tm), BF16)],
        compiler_params=_cparams("arbitrary", "arbitrary"),
        name="peer_dense",
    )(x, h2, mod, c1, e1, r2, e2, u, vt, final_g)


def kernel(x, c, rel_bias, w_mod, b_mod, norm1_g, norm2_g, w_in, w_out, diff_lambda, subln_g, conf_dw, conf_ln_g, conf_ln_b, sconv_w, sgu_ln_g, sgu_ln_b, sgu_w, sgu_b, peer_wq, peer_keys, peer_u, peer_v, final_g):
    batch, seq, d = x.shape
    depth = w_in.shape[0]
    t = batch * seq
    assert w_in.shape[2] == N_SLICES * GROUP and seq % 512 == 0 and d % LANES == 0

    xf = x.reshape(t, d)
    mods = _mod_call(c, w_mod, b_mod).reshape(depth, batch, 1, N_MOD * d)
    bias_diag, bias_off, bias_far = _attn_bias_tiles(rel_bias)
    row = lambda v: v.reshape(1, -1)

    for l in range(depth):
        lam_init = 0.8 - 0.6 * math.exp(-0.3 * l)
        qkv, pc = _in_call(xf, mods[l], row(norm1_g[l]), w_in[l].astype(BF16), seq)
        ya = _attn_call(qkv, bias_diag, bias_off, bias_far, diff_lambda[l],
                        row(jnp.tile(subln_g[l], ATT_HEADS)), batch, seq, lam_init)
        sgu_bias_tile = jnp.repeat(jnp.transpose(sgu_b[l]), GROUP // SGU_HEADS, axis=1)
        ybcd = _conv_call(pc, conf_dw[l], row(conf_ln_g[l]), row(conf_ln_b[l]), sconv_w[l],
                          row(sgu_ln_g[l]), row(sgu_ln_b[l]), sgu_w[l], sgu_bias_tile, batch, seq)
        keys = peer_keys[l].reshape(2 * PEER_HEADS, NKEYS, -1).astype(BF16)
        xf, h2, c1, e1, r2, e2 = _outtop_call(xf, ya, ybcd, mods[l], w_out[l].astype(BF16),
                                              row(norm2_g[l]), peer_wq[l].astype(BF16), keys, seq)
        xf = _peer_call(xf, h2, mods[l], c1, e1, r2, e2, peer_u[l].astype(BF16),
                        jnp.transpose(peer_v[l]).astype(BF16), row(final_g), seq, l == depth - 1)
    return xf.reshape(batch, seq, d)
```

```python
import functools
import math

import jax
import jax.numpy as jnp
import numpy as np
from jax import lax
from jax.experimental import pallas as pl
from jax.experimental.pallas import tpu as pltpu

F32 = jnp.float32
BF16 = jnp.bfloat16
I32 = jnp.int32
HIGHEST = lax.Precision.HIGHEST

GROUP = 256
N_SLICES = 10
ATT_HEADS = 4
ATT_V = 64
ATT_QK = 32
N_MAPS = 2 * ATT_HEADS
NUM_BUCKETS = 32
MAX_DISTANCE = 128
CONF_W = 31
SCONV_W = 3
SGU_HEADS = 4
SGU_CHUNK = 128
PEER_HEADS = 8
NKEYS = 128
TOPK = 16
N_MOD = 6
EPS = 1e-6
NEG_INF = -1e30
SQRT_HALF = float(np.sqrt(0.5))

LANES = 128
SUBLANES = 8
VMEM_LIMIT_BYTES = 56 * 1024 * 1024

ATQ = 256
ONES_ROWS = 16

CAND_COUNTS = [TOPK // (a + 1) for a in range(TOPK)]
CAND_STARTS = [int(v) for v in np.cumsum([0] + CAND_COUNTS[:-1])]
N_CAND = sum(CAND_COUNTS)
N_CAND_PAD = -(-N_CAND // SUBLANES) * SUBLANES

_NT = (((1,), (1,)), ((), ()))
_TN = (((0,), (0,)), ((), ()))


def _gelu(x):
    return 0.5 * x * (1.0 + lax.erf(x * SQRT_HALF))


def _cparams(*sem):
    return pltpu.CompilerParams(dimension_semantics=sem, vmem_limit_bytes=VMEM_LIMIT_BYTES)


def _mod_kernel(c_ref, w_ref, b_ref, o_ref):
    c = c_ref[...]
    cond = c * jax.nn.sigmoid(c)
    o_ref[0] = jnp.dot(cond, w_ref[0], precision=HIGHEST, preferred_element_type=F32) + b_ref[0]


def _mod_call(c, w_mod, b_mod):
    depth, d, n = w_mod.shape
    b = c.shape[0]
    tn = 2048
    return pl.pallas_call(
        _mod_kernel,
        grid=(depth, n // tn),
        in_specs=[pl.BlockSpec((b, d), lambda l, j: (0, 0)),
                  pl.BlockSpec((1, d, tn), lambda l, j: (l, 0, j)),
                  pl.BlockSpec((1, 1, tn), lambda l, j: (l, 0, j))],
        out_specs=pl.BlockSpec((1, b, tn), lambda l, j: (l, 0, j)),
        out_shape=jax.ShapeDtypeStruct((depth, b, n), F32),
        compiler_params=_cparams("arbitrary", "arbitrary"),
        name="mod",
    )(c, w_mod, b_mod.reshape(depth, 1, n))


def _modulated_norm(x, g, shift, scale):
    ms = jnp.mean(x * x, axis=-1, keepdims=True)
    return (x * lax.rsqrt(ms + EPS) * g) * (1.0 + scale) + shift


def _in_kernel(x_ref, mod_ref, g_ref, wqv_ref, wr_ref, qt_ref, k_ref, vt_ref, pc_ref):
    d = x_ref.shape[1]
    h = _modulated_norm(x_ref[...], g_ref[...], mod_ref[0, :, 0:d], mod_ref[0, :, d:2 * d]).astype(BF16)
    qv = lax.dot_general(wqv_ref[...], h, _NT, preferred_element_type=F32)
    qt_ref[...] = (qv[0:GROUP] * (ATT_QK ** -0.5)).astype(BF16)
    for c in range(vt_ref.shape[0]):
        vt_ref[c] = qv[GROUP:2 * GROUP, c * ATQ:(c + 1) * ATQ].astype(BF16)
    res = jnp.dot(h, wr_ref[...], preferred_element_type=F32)
    k_ref[...] = res[:, 0:GROUP].astype(BF16)
    pc_ref[...] = res[:, GROUP:]


def _in_call(x, mod, g, wqv_t, w_rest, seq):
    t, d = x.shape
    n = w_rest.shape[1]
    tm = 512
    return pl.pallas_call(
        _in_kernel,
        grid=(t // tm,),
        in_specs=[pl.BlockSpec((tm, d), lambda i: (i, 0)),
                  pl.BlockSpec((1, 1, N_MOD * d), lambda i: ((i * tm) // seq, 0, 0)),
                  pl.BlockSpec((1, d), lambda i: (0, 0)),
                  pl.BlockSpec((2 * GROUP, d), lambda i: (0, 0)),
                  pl.BlockSpec((d, n), lambda i: (0, 0))],
        out_specs=[pl.BlockSpec((GROUP, tm), lambda i: (0, i)),
                   pl.BlockSpec((tm, GROUP), lambda i: (i, 0)),
                   pl.BlockSpec((tm // ATQ, GROUP, ATQ), lambda i: (i, 0, 0)),
                   pl.BlockSpec((tm, n - GROUP), lambda i: (i, 0))],
        out_shape=[jax.ShapeDtypeStruct((GROUP, t), BF16),
                   jax.ShapeDtypeStruct((t, GROUP), BF16),
                   jax.ShapeDtypeStruct((t // ATQ, GROUP, ATQ), BF16),
                   jax.ShapeDtypeStruct((t, n - GROUP), F32)],
        compiler_params=_cparams("arbitrary"),
        name="in_proj",
    )(x, mod, g, wqv_t, w_rest)


def _attn_kernel(qt_ref, k_ref, vt_ref, bias_ref, lam_ref, g_ref, o_ref, qz_ref, acc_ref, m_ref, *, lam_init):
    i = pl.program_id(1)
    qt = qt_ref[...]
    rowi = lax.broadcasted_iota(I32, (GROUP, ATQ), 0)
    for c in range(N_MAPS):
        lo = c * ATT_QK
        qz_ref[c] = jnp.where((rowi >= lo) & (rowi < lo + ATT_QK), qt, jnp.zeros_like(qt))
    ones = jnp.ones((ONES_ROWS, ATQ), BF16)

    def step(jb, bias_idx, init):
        kb = k_ref[pl.ds(pl.multiple_of(jb * ATQ, ATQ), ATQ), :]
        vb = vt_ref[jb]
        m_old = None if init else m_ref[...]
        acc_old = None if init else [acc_ref[c] for c in range(N_MAPS)]
        scores = [jnp.dot(kb, qz_ref[c], preferred_element_type=F32) for c in range(N_MAPS)]
        m_new, probs, alphas = [], [], []
        for c in range(N_MAPS):
            s = scores[c]
            if bias_idx is not None:
                s = s + bias_ref[c // 2, bias_idx]
            mn = jnp.max(s, axis=0, keepdims=True)
            if not init:
                mo = m_old[c:c + 1, :]
                mn = jnp.maximum(mo, mn)
                alphas.append(jnp.exp(mo - mn))
            probs.append(jnp.exp(s - mn).astype(BF16))
            m_new.append(mn)
        acc_new = []
        for c in range(N_MAPS):
            h = c // 2
            vaug = jnp.concatenate([vb[h * ATT_V:(h + 1) * ATT_V, :], ones], axis=0)
            pv = jnp.dot(vaug, probs[c], preferred_element_type=F32)
            acc_new.append(pv if init else alphas[c] * acc_old[c] + pv)
        m_ref[...] = jnp.concatenate(m_new, axis=0)
        for c in range(N_MAPS):
            acc_ref[c] = acc_new[c]

    step(i, 1, True)

    @pl.when(i >= 1)
    def _():
        step(i - 1, 0, False)

    def far(j, carry):
        step(j, None, False)
        return carry

    lax.fori_loop(0, jnp.maximum(i - 1, 0), far, 0)

    lp = lam_ref[...]
    lam = (jnp.exp(jnp.sum(lp[0:1] * lp[1:2], axis=1, keepdims=True))
           - jnp.exp(jnp.sum(lp[2:3] * lp[3:4], axis=1, keepdims=True)) + lam_init)
    for h in range(ATT_HEADS):
        a0 = acc_ref[2 * h]
        a1 = acc_ref[2 * h + 1]
        o = a0[0:ATT_V] / a0[ATT_V:ATT_V + 1] - lam * (a1[0:ATT_V] / a1[ATT_V:ATT_V + 1])
        ms = jnp.mean(o * o, axis=0, keepdims=True)
        y = o * lax.rsqrt(ms + EPS) * g_ref[...] * (1.0 - lam_init)
        o_ref[h * ATT_V:(h + 1) * ATT_V, :] = y.astype(o_ref.dtype)


def _attn_call(qt, k, vt, bias, lam_par, subln_col, batch, seq, lam_init):
    t = k.shape[0]
    nq = seq // ATQ
    return pl.pallas_call(
        functools.partial(_attn_kernel, lam_init=lam_init),
        grid=(batch, nq),
        in_specs=[pl.BlockSpec((GROUP, ATQ), lambda b, i: (0, b * nq + i)),
                  pl.BlockSpec((seq, GROUP), lambda b, i: (b, 0)),
                  pl.BlockSpec((nq, GROUP, ATQ), lambda b, i: (b, 0, 0)),
                  pl.BlockSpec((ATT_HEADS, 2, ATQ, ATQ), lambda b, i: (0, 0, 0, 0)),
                  pl.BlockSpec((4, ATT_QK), lambda b, i: (0, 0)),
                  pl.BlockSpec((ATT_V, 1), lambda b, i: (0, 0))],
        out_specs=pl.BlockSpec((GROUP, ATQ), lambda b, i: (0, b * nq + i)),
        out_shape=jax.ShapeDtypeStruct((GROUP, t), BF16),
        scratch_shapes=[pltpu.VMEM((N_MAPS, GROUP, ATQ), BF16),
                        pltpu.VMEM((N_MAPS, ATT_V + ONES_ROWS, ATQ), F32),
                        pltpu.VMEM((N_MAPS, ATQ), F32)],
        compiler_params=_cparams("arbitrary", "arbitrary"),
        name="diff_attn",
    )(qt, k, vt, bias, lam_par, subln_col)


def _rel_bucket_np(n):
    max_exact = NUM_BUCKETS // 2
    n = np.maximum(n, 0)
    ratio = (np.log(np.maximum(n, 1).astype(np.float32) / np.float32(max_exact))
             / np.float32(math.log(MAX_DISTANCE / max_exact)))
    large = np.minimum(max_exact + (ratio * (NUM_BUCKETS - max_exact)).astype(np.int32), NUM_BUCKETS - 1)
    return np.where(n < max_exact, n, large).astype(np.int32)


def _attn_bias_tiles(rel_bias):
    rel_cur = np.arange(ATQ)[None, :] - np.arange(ATQ)[:, None]
    rel_prev = rel_cur + ATQ
    assert np.all(_rel_bucket_np(np.arange(ATQ + 1, 8 * ATQ)) == NUM_BUCKETS - 1)
    far = rel_bias[NUM_BUCKETS - 1]
    prev = jnp.transpose(rel_bias[_rel_bucket_np(rel_prev)] - far, (2, 0, 1))
    cur = jnp.transpose(rel_bias[_rel_bucket_np(rel_cur)] - far, (2, 0, 1))
    cur = jnp.where(jnp.asarray(rel_cur >= 0)[None], cur, NEG_INF)
    return jnp.stack([prev, cur], axis=1).astype(F32)


CONV_HALO = 32
SCONV_HALO = 8


def _layer_norm(x, g, b):
    mu = jnp.mean(x, axis=-1, keepdims=True)
    xc = x - mu
    var = jnp.mean(xc * xc, axis=-1, keepdims=True)
    return xc * lax.rsqrt(var + EPS) * g + b


def _conv_kernel(cur_ref, hga_ref, hgb_ref, hcc_ref, hch_ref, dw_ref, lng_ref, lnb_ref, sw_ref,
                 sg_ref, sb_ref, ws_ref, bs_ref, o_ref, ext_ref, ext2_ref):
    i = pl.program_id(1)
    ts = cur_ref.shape[0]
    first = i == 0

    def col(k):
        return cur_ref[:, k * GROUP:(k + 1) * GROUP]

    zh = hga_ref[...] * jax.nn.sigmoid(hgb_ref[...])
    ext_ref[0:CONV_HALO, :] = jnp.where(first, 0.0, zh)
    ext_ref[CONV_HALO:, :] = col(0) * jax.nn.sigmoid(col(1))
    acc = jnp.zeros((ts, GROUP), F32)
    for j in range(CONF_W):
        off = CONV_HALO - (CONF_W - 1) + j
        acc = acc + dw_ref[j:j + 1, :] * ext_ref[off:off + ts, :]
    zb = _layer_norm(acc, lng_ref[...], lnb_ref[...])
    o_ref[:, 0:GROUP] = (zb * jax.nn.sigmoid(zb)).astype(o_ref.dtype)

    ext2_ref[0:SCONV_HALO, :] = jnp.where(first, 0.0, hcc_ref[...] * hch_ref[...])
    ext2_ref[SCONV_HALO:, :] = col(3) * col(4)
    acc = jnp.zeros((ts, GROUP), F32)
    for j in range(SCONV_W):
        off = SCONV_HALO - (SCONV_W - 1) + j
        acc = acc + sw_ref[j:j + 1, :] * ext2_ref[off:off + ts, :]
    o_ref[:, GROUP:2 * GROUP] = (col(2) * acc).astype(o_ref.dtype)

    lane = lax.broadcasted_iota(I32, (1, GROUP), 1)
    rr = lax.broadcasted_iota(I32, (SGU_CHUNK, SGU_CHUNK), 0)
    cc = lax.broadcasted_iota(I32, (SGU_CHUNK, SGU_CHUNK), 1)
    hd = GROUP // SGU_HEADS
    w_tril = [jnp.where(rr >= cc, ws_ref[h], 0.0).astype(BF16) for h in range(SGU_HEADS)]
    for c in range(ts // SGU_CHUNK):
        rows = slice(c * SGU_CHUNK, (c + 1) * SGU_CHUNK)
        u = _gelu(cur_ref[rows, 5 * GROUP:6 * GROUP])
        v = _layer_norm(_gelu(cur_ref[rows, 6 * GROUP:7 * GROUP]), sg_ref[...], sb_ref[...]).astype(BF16)
        z = bs_ref[...]
        for h in range(SGU_HEADS):
            zh_ = jnp.dot(w_tril[h], v, preferred_element_type=F32)
            z = z + jnp.where((lane >= h * hd) & (lane < (h + 1) * hd), zh_, 0.0)
        o_ref[rows, 2 * GROUP:3 * GROUP] = (u * z).astype(o_ref.dtype)


def _conv_call(pc, conf_dw, conf_ln_g, conf_ln_b, sconv_w, sgu_ln_g, sgu_ln_b, sgu_w, sgu_bias_tile,
               batch, seq):
    t, n = pc.shape
    ts = 512
    nst = seq // ts

    def halo_map(rows, colblk):
        per = ts // rows
        return lambda b, i: (jnp.maximum((b * nst + i) * per - 1, 0), colblk)

    vec = lambda: pl.BlockSpec((1, GROUP), lambda b, i: (0, 0))
    return pl.pallas_call(
        _conv_kernel,
        grid=(batch, nst),
        in_specs=[pl.BlockSpec((ts, n), lambda b, i: (b * nst + i, 0)),
                  pl.BlockSpec((CONV_HALO, GROUP), halo_map(CONV_HALO, 0)),
                  pl.BlockSpec((CONV_HALO, GROUP), halo_map(CONV_HALO, 1)),
                  pl.BlockSpec((SCONV_HALO, GROUP), halo_map(SCONV_HALO, 3)),
                  pl.BlockSpec((SCONV_HALO, GROUP), halo_map(SCONV_HALO, 4)),
                  pl.BlockSpec((CONF_W, GROUP), lambda b, i: (0, 0)),
                  vec(), vec(),
                  pl.BlockSpec((SCONV_W, GROUP), lambda b, i: (0, 0)),
                  vec(), vec(),
                  pl.BlockSpec((SGU_HEADS, SGU_CHUNK, SGU_CHUNK), lambda b, i: (0, 0, 0)),
                  pl.BlockSpec((SGU_CHUNK, GROUP), lambda b, i: (0, 0))],
        out_specs=pl.BlockSpec((ts, 3 * GROUP), lambda b, i: (b * nst + i, 0)),
        out_shape=jax.ShapeDtypeStruct((t, 3 * GROUP), BF16),
        scratch_shapes=[pltpu.VMEM((CONV_HALO + ts, GROUP), F32),
                        pltpu.VMEM((SCONV_HALO + ts, GROUP), F32)],
        compiler_params=_cparams("arbitrary", "arbitrary"),
        name="conv_mix",
    )(pc, pc, pc, pc, pc, conf_dw, conf_ln_g, conf_ln_b, sconv_w, sgu_ln_g, sgu_ln_b, sgu_w,
      sgu_bias_tile)


def _extract_top(s, rowi, n_rounds, want_rank):
    sentinel = float(s.shape[0])
    vals = []
    mark = jnp.full(s.shape, float(n_rounds) if want_rank else 0.0, F32)
    for r in range(n_rounds):
        m = jnp.max(s, axis=0, keepdims=True)
        idx = jnp.min(jnp.where(s == m, rowi, sentinel), axis=0, keepdims=True)
        hit = rowi == idx
        mark = jnp.where(hit, float(r) if want_rank else 1.0, mark)
        s = jnp.where(hit, -jnp.inf, s)
        vals.append(m)
    return vals, mark


def _pack_pair(x):
    bits = pltpu.bitcast(x.astype(BF16).astype(F32), jnp.uint32)
    return bits | (bits >> 16)


def _outtop_kernel(x_ref, yat_ref, yb_ref, mod_ref, wo_ref, g_ref, wq_ref, keys_ref,
                   xo_ref, h2_ref, c1_ref, e1_ref, r2_ref, e2_ref, qs_ref):
    d = x_ref.shape[1]
    tm = x_ref.shape[0]
    y = lax.dot_general(yat_ref[...], wo_ref[0:GROUP, :], _TN, preferred_element_type=F32)
    y = y + jnp.dot(yb_ref[...], wo_ref[GROUP:, :], preferred_element_type=F32)
    x = x_ref[...] + mod_ref[0, :, 2 * d:3 * d] * y
    xo_ref[...] = x
    h2 = _modulated_norm(x, g_ref[...], mod_ref[0, :, 3 * d:4 * d], mod_ref[0, :, 4 * d:5 * d]).astype(BF16)
    h2_ref[...] = h2
    q = jnp.dot(h2, wq_ref[...], preferred_element_type=F32).astype(BF16)
    for hp in range(2 * PEER_HEADS):
        qs_ref[hp] = q[:, hp * NKEYS:(hp + 1) * NKEYS]

    rowi = lax.broadcasted_iota(I32, (NKEYS, tm), 0).astype(F32)
    rowc = lax.broadcasted_iota(I32, (N_CAND_PAD, tm), 0)
    rowc_f = rowc.astype(F32)
    a_row = jnp.zeros((N_CAND_PAD, tm), I32) - 1
    for st in CAND_STARTS:
        a_row = a_row + (rowc >= st).astype(I32)
    a_row = jnp.where(rowc >= N_CAND, TOPK, a_row)
    start_row = jnp.zeros((N_CAND_PAD, tm), I32)
    for a, st in enumerate(CAND_STARTS):
        start_row = jnp.where(a_row == a, st, start_row)
    b_row = rowc - start_row

    def head(hh, carry):
        halves = []
        for p in range(2):
            s = lax.dot_general(keys_ref[2 * hh + p], qs_ref[2 * hh + p], _NT,
                                preferred_element_type=F32)
            vals, rank = _extract_top(s, rowi, TOPK, True)
            halves.append((s, vals, rank))
        (s1, v1, r1), (s2, v2, r2) = halves
        rep1 = jnp.full((N_CAND_PAD, tm), -jnp.inf, F32)
        rep2 = jnp.full((N_CAND_PAD, tm), -jnp.inf, F32)
        for a in range(TOPK):
            rep1 = jnp.where(a_row == a, v1[a], rep1)
            rep2 = jnp.where(b_row == a, v2[a], rep2)
        cand = rep1 + rep2
        _, sel = _extract_top(cand, rowc_f, TOPK, False)
        cmax = v1[0] + v2[0]
        z = jnp.sum(sel * jnp.exp(jnp.where(sel > 0.0, cand, cmax) - cmax), axis=0, keepdims=True)
        c1 = jnp.zeros((NKEYS, tm), F32)
        for a in range(TOPK):
            cnt = jnp.sum(jnp.where(a_row == a, sel, 0.0), axis=0, keepdims=True)
            c1 = jnp.where(r1 == float(a), cnt, c1)
        c1_ref[hh] = _pack_pair(c1)
        e1_ref[hh] = _pack_pair(jnp.exp(s1 - v1[0]) / z)
        r2_ref[hh] = r2.astype(BF16)
        e2_ref[hh] = jnp.exp(s2 - v2[0]).astype(BF16)
        return carry

    lax.fori_loop(0, PEER_HEADS, head, 0)


def _outtop_call(x, yat, ybcd, mod, w_out, g2, wq, keys, seq):
    t, d = x.shape
    tm = 256
    nq = wq.shape[1]
    tab = lambda: pl.BlockSpec((PEER_HEADS, NKEYS, tm), lambda i: (0, 0, i))
    pair_shape = jax.ShapeDtypeStruct((PEER_HEADS, NKEYS, t), jnp.uint32)
    half_shape = jax.ShapeDtypeStruct((PEER_HEADS, NKEYS, t), BF16)
    return pl.pallas_call(
        _outtop_kernel,
        grid=(t // tm,),
        in_specs=[pl.BlockSpec((tm, d), lambda i: (i, 0)),
                  pl.BlockSpec((GROUP, tm), lambda i: (0, i)),
                  pl.BlockSpec((tm, 3 * GROUP), lambda i: (i, 0)),
                  pl.BlockSpec((1, 1, N_MOD * d), lambda i: ((i * tm) // seq, 0, 0)),
                  pl.BlockSpec((4 * GROUP, d), lambda i: (0, 0)),
                  pl.BlockSpec((1, d), lambda i: (0, 0)),
                  pl.BlockSpec((d, nq), lambda i: (0, 0)),
                  pl.BlockSpec((2 * PEER_HEADS, NKEYS, NKEYS), lambda i: (0, 0, 0))],
        out_specs=[pl.BlockSpec((tm, d), lambda i: (i, 0)),
                   pl.BlockSpec((tm, d), lambda i: (i, 0)),
                   tab(), tab(), tab(), tab()],
        out_shape=[jax.ShapeDtypeStruct((t, d), F32),
                   jax.ShapeDtypeStruct((t, d), BF16),
                   pair_shape, pair_shape, half_shape, half_shape],
        scratch_shapes=[pltpu.VMEM((2 * PEER_HEADS, tm, NKEYS), BF16)],
        compiler_params=_cparams("arbitrary"),
        name="out_topk",
    )(x, yat, ybcd, mod, w_out, g2, wq, keys)


def _bcast_pair_rows(row_u32, n_rows):
    return pltpu.bitcast(jnp.broadcast_to(row_u32, (n_rows // 2, row_u32.shape[1])), BF16)


def _peer_kernel(h2_ref, u_ref, x_ref, mod_ref, c1_ref, e1_ref, r2_ref, e2_ref, vt_ref, fg_ref,
                 o_ref, acc_ref, a0_ref, a1_ref, *, nk, final):
    s = pl.program_id(0)
    kp = jnp.maximum(s - 1, 0) % nk
    nb = u_ref.shape[0] // NKEYS
    d = x_ref.shape[1]

    @pl.when(s == 0)
    def _():
        a1_ref[...] = jnp.zeros_like(a1_ref)

    @pl.when(kp == 0)
    def _():
        acc_ref[...] = jnp.zeros_like(acc_ref)

    def body(a_cur, a_prev):
        a_cur[...] = lax.dot_general(u_ref[...], h2_ref[...], _NT, preferred_element_type=F32)
        blocks = []
        for ii in range(nb):
            i1 = kp * nb + ii
            act = _gelu(a_prev[ii * NKEYS:(ii + 1) * NKEYS, :]).astype(BF16)
            gate = None
            for hh in range(PEER_HEADS):
                c1 = _bcast_pair_rows(c1_ref[hh, pl.ds(i1, 1), :], NKEYS)
                e1 = _bcast_pair_rows(e1_ref[hh, pl.ds(i1, 1), :], NKEYS)
                g = jnp.where(r2_ref[hh] < c1, e2_ref[hh] * e1, jnp.zeros_like(e1))
                gate = g if gate is None else gate + g
            blocks.append(gate * act)
        w = jnp.concatenate(blocks, axis=0)
        acc_ref[...] += jnp.dot(vt_ref[...], w, preferred_element_type=F32)

    @pl.when(s % 2 == 0)
    def _():
        body(a0_ref, a1_ref)

    @pl.when(s % 2 == 1)
    def _():
        body(a1_ref, a0_ref)

    @pl.when((kp == nk - 1) & (s >= 1))
    def _():
        y = jnp.transpose(acc_ref[...])
        x = x_ref[...] + mod_ref[0, :, 5 * d:6 * d] * y
        if final:
            ms = jnp.mean(x * x, axis=-1, keepdims=True)
            x = x * lax.rsqrt(ms + EPS) * fg_ref[...]
        o_ref[...] = x


def _peer_call(x, h2, mod, c1, e1, r2, e2, u, vt, final_g, seq, final):
    t, d = x.shape
    ne = u.shape[0]
    tm = 512
    ec = 1024
    n_i, nk = t // tm, ne // ec
    last = n_i * nk - 1
    cur_i = lambda s: jnp.minimum(s, last) // nk
    cur_k = lambda s: jnp.minimum(s, last) % nk
    prev_i = lambda s: jnp.maximum(s - 1, 0) // nk
    prev_k = lambda s: jnp.maximum(s - 1, 0) % nk
    tab = lambda: pl.BlockSpec((PEER_HEADS, NKEYS, tm), lambda s: (0, 0, prev_i(s)))
    return pl.pallas_call(
        functools.partial(_peer_kernel, nk=nk, final=final),
        grid=(n_i * nk + 1,),
        in_specs=[pl.BlockSpec((tm, d), lambda s: (cur_i(s), 0)),
                  pl.BlockSpec((ec, d), lambda s: (cur_k(s), 0)),
                  pl.BlockSpec((tm, d), lambda s: (prev_i(s), 0)),
                  pl.BlockSpec((1, 1, N_MOD * d), lambda s: ((prev_i(s) * tm) // seq, 0, 0)),
                  tab(), tab(), tab(), tab(),
                  pl.BlockSpec((d, ec), lambda s: (0, prev_k(s))),
                  pl.BlockSpec((1, d), lambda s: (0, 0))],
        out_specs=pl.BlockSpec((tm, d), lambda s: (prev_i(s), 0)),
        out_shape=jax.ShapeDtypeStruct((t, d), F32),
        scratch_shapes=[pltpu.VMEM((d, tm), F32),
                        pltpu.VMEM((ec, tm), F32),
                        pltpu.VMEM((ec, tm), F32)],
        compiler_params=_cparams("arbitrary"),
        name="peer_dense",
    )(h2, u, x, mod, c1, e1, r2, e2, vt, final_g)


def kernel(x, c, rel_bias, w_mod, b_mod, norm1_g, norm2_g, w_in, w_out, diff_lambda, subln_g, conf_dw, conf_ln_g, conf_ln_b, sconv_w, sgu_ln_g, sgu_ln_b, sgu_w, sgu_b, peer_wq, peer_keys, peer_u, peer_v, final_g):
    batch, seq, d = x.shape
    depth = w_in.shape[0]
    t = batch * seq
    assert w_in.shape[2] == N_SLICES * GROUP and seq % 512 == 0 and d % LANES == 0

    xf = x.reshape(t, d)
    mods = _mod_call(c, w_mod, b_mod).reshape(depth, batch, 1, N_MOD * d)
    attn_bias = _attn_bias_tiles(rel_bias)
    row = lambda v: v.reshape(1, -1)

    for l in range(depth):
        lam_init = 0.8 - 0.6 * math.exp(-0.3 * l)
        wl = w_in[l]
        wqv_t = jnp.transpose(jnp.concatenate([wl[:, 0:GROUP], wl[:, 2 * GROUP:3 * GROUP]], axis=1)).astype(BF16)
        w_rest = jnp.concatenate([wl[:, GROUP:2 * GROUP], wl[:, 3 * GROUP:]], axis=1).astype(BF16)
        qt, k, vt, pc = _in_call(xf, mods[l], row(norm1_g[l]), wqv_t, w_rest, seq)
        yat = _attn_call(qt, k, vt, attn_bias, diff_lambda[l], subln_g[l].reshape(ATT_V, 1),
                         batch, seq, lam_init)
        sgu_bias_tile = jnp.repeat(jnp.transpose(sgu_b[l]), GROUP // SGU_HEADS, axis=1)
        ybcd = _conv_call(pc, conf_dw[l], row(conf_ln_g[l]), row(conf_ln_b[l]), sconv_w[l],
                          row(sgu_ln_g[l]), row(sgu_ln_b[l]), sgu_w[l], sgu_bias_tile, batch, seq)
        keys = peer_keys[l].reshape(2 * PEER_HEADS, NKEYS, -1).astype(BF16)
        xf, h2, c1, e1, r2, e2 = _outtop_call(xf, yat, ybcd, mods[l], w_out[l].astype(BF16),
                                              row(norm2_g[l]), peer_wq[l].astype(BF16), keys, seq)
        xf = _peer_call(xf, h2, mods[l], c1, e1, r2, e2, peer_u[l].astype(BF16),
                        jnp.transpose(peer_v[l]).astype(BF16), row(final_g), seq, l == depth - 1)
    return xf.reshape(batch, seq, d)
```

```python
import functools
import math

import jax
import jax.numpy as jnp
import numpy as np
from jax import lax
from jax.experimental import pallas as pl
from jax.experimental.pallas import tpu as pltpu

F32 = jnp.float32
BF16 = jnp.bfloat16
I32 = jnp.int32
HIGHEST = lax.Precision.HIGHEST

GROUP = 256
N_SLICES = 10
ATT_HEADS = 4
ATT_V = 64
ATT_QK = 32
N_MAPS = 2 * ATT_HEADS
NUM_BUCKETS = 32
MAX_DISTANCE = 128
CONF_W = 31
SCONV_W = 3
SGU_HEADS = 4
SGU_CHUNK = 128
PEER_HEADS = 8
NKEYS = 128
TOPK = 16
N_MOD = 6
EPS = 1e-6
NEG_INF = -1e30
SQRT_HALF = float(np.sqrt(0.5))

LANES = 128
SUBLANES = 8
VMEM_LIMIT_BYTES = 56 * 1024 * 1024

PEER_TM = 1024
PEER_EC = 1024
PEER_TN = 256
ATQ = 256
ONES_ROWS = 16

CAND_COUNTS = [TOPK // (a + 1) for a in range(TOPK)]
CAND_STARTS = [int(v) for v in np.cumsum([0] + CAND_COUNTS[:-1])]
N_CAND = sum(CAND_COUNTS)
N_CAND_PAD = -(-N_CAND // SUBLANES) * SUBLANES

_NT = (((1,), (1,)), ((), ()))
_TN = (((0,), (0,)), ((), ()))


def _gelu(x):
    return 0.5 * x * (1.0 + lax.erf(x * SQRT_HALF))


def _cparams(*sem):
    return pltpu.CompilerParams(dimension_semantics=sem, vmem_limit_bytes=VMEM_LIMIT_BYTES)


def _mod_kernel(c_ref, w_ref, b_ref, o_ref):
    c = c_ref[...]
    cond = c * jax.nn.sigmoid(c)
    o_ref[0] = jnp.dot(cond, w_ref[0], precision=HIGHEST, preferred_element_type=F32) + b_ref[0]


def _mod_call(c, w_mod, b_mod):
    depth, d, n = w_mod.shape
    b = c.shape[0]
    tn = 2048
    return pl.pallas_call(
        _mod_kernel,
        grid=(depth, n // tn),
        in_specs=[pl.BlockSpec((b, d), lambda l, j: (0, 0)),
                  pl.BlockSpec((1, d, tn), lambda l, j: (l, 0, j)),
                  pl.BlockSpec((1, 1, tn), lambda l, j: (l, 0, j))],
        out_specs=pl.BlockSpec((1, b, tn), lambda l, j: (l, 0, j)),
        out_shape=jax.ShapeDtypeStruct((depth, b, n), F32),
        compiler_params=_cparams("arbitrary", "arbitrary"),
        name="mod",
    )(c, w_mod, b_mod.reshape(depth, 1, n))


def _modulated_norm(x, g, shift, scale):
    ms = jnp.mean(x * x, axis=-1, keepdims=True)
    return (x * lax.rsqrt(ms + EPS) * g) * (1.0 + scale) + shift


def _in_kernel(x_ref, mod_ref, g_ref, wqv_ref, wr_ref, qt_ref, k_ref, vt_ref, pc_ref):
    d = x_ref.shape[1]
    h = _modulated_norm(x_ref[...], g_ref[...], mod_ref[0, :, 0:d], mod_ref[0, :, d:2 * d]).astype(BF16)
    qv = lax.dot_general(wqv_ref[...], h, _NT, preferred_element_type=F32)
    qt_ref[...] = (qv[0:GROUP] * (ATT_QK ** -0.5)).astype(BF16)
    for c in range(vt_ref.shape[0]):
        vt_ref[c] = qv[GROUP:2 * GROUP, c * ATQ:(c + 1) * ATQ].astype(BF16)
    res = jnp.dot(h, wr_ref[...], preferred_element_type=F32)
    k_ref[...] = res[:, 0:GROUP].astype(BF16)
    pc_ref[...] = res[:, GROUP:]


def _in_call(x, mod, g, wqv_t, w_rest, seq):
    t, d = x.shape
    n = w_rest.shape[1]
    tm = 512
    return pl.pallas_call(
        _in_kernel,
        grid=(t // tm,),
        in_specs=[pl.BlockSpec((tm, d), lambda i: (i, 0)),
                  pl.BlockSpec((1, 1, N_MOD * d), lambda i: ((i * tm) // seq, 0, 0)),
                  pl.BlockSpec((1, d), lambda i: (0, 0)),
                  pl.BlockSpec((2 * GROUP, d), lambda i: (0, 0)),
                  pl.BlockSpec((d, n), lambda i: (0, 0))],
        out_specs=[pl.BlockSpec((GROUP, tm), lambda i: (0, i)),
                   pl.BlockSpec((tm, GROUP), lambda i: (i, 0)),
                   pl.BlockSpec((tm // ATQ, GROUP, ATQ), lambda i: (i, 0, 0)),
                   pl.BlockSpec((tm, n - GROUP), lambda i: (i, 0))],
        out_shape=[jax.ShapeDtypeStruct((GROUP, t), BF16),
                   jax.ShapeDtypeStruct((t, GROUP), BF16),
                   jax.ShapeDtypeStruct((t // ATQ, GROUP, ATQ), BF16),
                   jax.ShapeDtypeStruct((t, n - GROUP), F32)],
        compiler_params=_cparams("arbitrary"),
        name="in_proj",
    )(x, mod, g, wqv_t, w_rest)


def _attn_kernel(qt_ref, k_ref, vt_ref, bias_ref, lam_ref, g_ref, o_ref, qz_ref, acc_ref, m_ref, *, lam_init):
    i = pl.program_id(1)
    qt = qt_ref[...]
    rowi = lax.broadcasted_iota(I32, (GROUP, ATQ), 0)
    for c in range(N_MAPS):
        lo = c * ATT_QK
        qz_ref[c] = jnp.where((rowi >= lo) & (rowi < lo + ATT_QK), qt, jnp.zeros_like(qt))
    ones = jnp.ones((ONES_ROWS, ATQ), BF16)

    def step(jb, bias_idx, init):
        kb = k_ref[pl.ds(pl.multiple_of(jb * ATQ, ATQ), ATQ), :]
        vb = vt_ref[jb]
        m_old = None if init else m_ref[...]
        acc_old = None if init else [acc_ref[c] for c in range(N_MAPS)]
        scores = [jnp.dot(kb, qz_ref[c], preferred_element_type=F32) for c in range(N_MAPS)]
        m_new, probs, alphas = [], [], []
        for c in range(N_MAPS):
            s = scores[c]
            if bias_idx is not None:
                s = s + bias_ref[c // 2, bias_idx]
            mn = jnp.max(s, axis=0, keepdims=True)
            if not init:
                mo = m_old[c:c + 1, :]
                mn = jnp.maximum(mo, mn)
                alphas.append(jnp.exp(mo - mn))
            probs.append(jnp.exp(s - mn).astype(BF16))
            m_new.append(mn)
        acc_new = []
        for c in range(N_MAPS):
            h = c // 2
            vaug = jnp.concatenate([vb[h * ATT_V:(h + 1) * ATT_V, :], ones], axis=0)
            pv = jnp.dot(vaug, probs[c], preferred_element_type=F32)
            acc_new.append(pv if init else alphas[c] * acc_old[c] + pv)
        m_ref[...] = jnp.concatenate(m_new, axis=0)
        for c in range(N_MAPS):
            acc_ref[c] = acc_new[c]

    step(i, 1, True)

    @pl.when(i >= 1)
    def _():
        step(i - 1, 0, False)

    def far(j, carry):
        step(j, None, False)
        return carry

    lax.fori_loop(0, jnp.maximum(i - 1, 0), far, 0)

    lp = lam_ref[...]
    lam = (jnp.exp(jnp.sum(lp[0:1] * lp[1:2], axis=1, keepdims=True))
           - jnp.exp(jnp.sum(lp[2:3] * lp[3:4], axis=1, keepdims=True)) + lam_init)
    for h in range(ATT_HEADS):
        a0 = acc_ref[2 * h]
        a1 = acc_ref[2 * h + 1]
        o = a0[0:ATT_V] / a0[ATT_V:ATT_V + 1] - lam * (a1[0:ATT_V] / a1[ATT_V:ATT_V + 1])
        ms = jnp.mean(o * o, axis=0, keepdims=True)
        y = o * lax.rsqrt(ms + EPS) * g_ref[...] * (1.0 - lam_init)
        o_ref[h * ATT_V:(h + 1) * ATT_V, :] = y.astype(o_ref.dtype)


def _attn_call(qt, k, vt, bias, lam_par, subln_col, batch, seq, lam_init):
    t = k.shape[0]
    nq = seq // ATQ
    return pl.pallas_call(
        functools.partial(_attn_kernel, lam_init=lam_init),
        grid=(batch, nq),
        in_specs=[pl.BlockSpec((GROUP, ATQ), lambda b, i: (0, b * nq + i)),
                  pl.BlockSpec((seq, GROUP), lambda b, i: (b, 0)),
                  pl.BlockSpec((nq, GROUP, ATQ), lambda b, i: (b, 0, 0)),
                  pl.BlockSpec((ATT_HEADS, 2, ATQ, ATQ), lambda b, i: (0, 0, 0, 0)),
                  pl.BlockSpec((4, ATT_QK), lambda b, i: (0, 0)),
                  pl.BlockSpec((ATT_V, 1), lambda b, i: (0, 0))],
        out_specs=pl.BlockSpec((GROUP, ATQ), lambda b, i: (0, b * nq + i)),
        out_shape=jax.ShapeDtypeStruct((GROUP, t), BF16),
        scratch_shapes=[pltpu.VMEM((N_MAPS, GROUP, ATQ), BF16),
                        pltpu.VMEM((N_MAPS, ATT_V + ONES_ROWS, ATQ), F32),
                        pltpu.VMEM((N_MAPS, ATQ), F32)],
        compiler_params=_cparams("arbitrary", "arbitrary"),
        name="diff_attn",
    )(qt, k, vt, bias, lam_par, subln_col)


def _rel_bucket_np(n):
    max_exact = NUM_BUCKETS // 2
    n = np.maximum(n, 0)
    ratio = (np.log(np.maximum(n, 1).astype(np.float32) / np.float32(max_exact))
             / np.float32(math.log(MAX_DISTANCE / max_exact)))
    large = np.minimum(max_exact + (ratio * (NUM_BUCKETS - max_exact)).astype(np.int32), NUM_BUCKETS - 1)
    return np.where(n < max_exact, n, large).astype(np.int32)


def _bias_kernel(bucket_ref, rb_ref, o_ref):
    h = pl.program_id(0)
    far = rb_ref[NUM_BUCKETS - 1, h]
    for tile in range(2):
        b = bucket_ref[tile]
        acc = jnp.full(b.shape, NEG_INF, F32)
        for k in range(NUM_BUCKETS):
            acc = jnp.where(b == k, rb_ref[k, h] - far, acc)
        o_ref[0, tile] = acc


def _attn_bias_tiles(rel_bias):
    rel_cur = np.arange(ATQ)[None, :] - np.arange(ATQ)[:, None]
    assert np.all(_rel_bucket_np(np.arange(ATQ + 1, 8 * ATQ)) == NUM_BUCKETS - 1)
    buckets = np.stack([_rel_bucket_np(rel_cur + ATQ),
                        np.where(rel_cur >= 0, _rel_bucket_np(rel_cur), NUM_BUCKETS)]).astype(np.int32)
    return pl.pallas_call(
        _bias_kernel,
        grid=(ATT_HEADS,),
        in_specs=[pl.BlockSpec((2, ATQ, ATQ), lambda h: (0, 0, 0)),
                  pl.BlockSpec(memory_space=pltpu.SMEM)],
        out_specs=pl.BlockSpec((1, 2, ATQ, ATQ), lambda h: (h, 0, 0, 0)),
        out_shape=jax.ShapeDtypeStruct((ATT_HEADS, 2, ATQ, ATQ), F32),
        compiler_params=_cparams("arbitrary"),
        name="attn_bias",
    )(jnp.asarray(buckets), rel_bias)


CONV_HALO = 32
SCONV_HALO = 8


def _layer_norm(x, g, b):
    mu = jnp.mean(x, axis=-1, keepdims=True)
    xc = x - mu
    var = jnp.mean(xc * xc, axis=-1, keepdims=True)
    return xc * lax.rsqrt(var + EPS) * g + b


def _conv_kernel(cur_ref, hga_ref, hgb_ref, hcc_ref, hch_ref, dw_ref, lng_ref, lnb_ref, sw_ref,
                 sg_ref, sb_ref, ws_ref, bs_ref, o_ref, ext_ref, ext2_ref):
    i = pl.program_id(1)
    ts = cur_ref.shape[0]
    first = i == 0

    def col(k):
        return cur_ref[:, k * GROUP:(k + 1) * GROUP]

    zh = hga_ref[...] * jax.nn.sigmoid(hgb_ref[...])
    ext_ref[0:CONV_HALO, :] = jnp.where(first, 0.0, zh)
    ext_ref[CONV_HALO:, :] = col(0) * jax.nn.sigmoid(col(1))
    acc = jnp.zeros((ts, GROUP), F32)
    for j in range(CONF_W):
        off = CONV_HALO - (CONF_W - 1) + j
        acc = acc + dw_ref[j:j + 1, :] * ext_ref[off:off + ts, :]
    zb = _layer_norm(acc, lng_ref[...], lnb_ref[...])
    o_ref[:, 0:GROUP] = (zb * jax.nn.sigmoid(zb)).astype(o_ref.dtype)

    ext2_ref[0:SCONV_HALO, :] = jnp.where(first, 0.0, hcc_ref[...] * hch_ref[...])
    ext2_ref[SCONV_HALO:, :] = col(3) * col(4)
    acc = jnp.zeros((ts, GROUP), F32)
    for j in range(SCONV_W):
        off = SCONV_HALO - (SCONV_W - 1) + j
        acc = acc + sw_ref[j:j + 1, :] * ext2_ref[off:off + ts, :]
    o_ref[:, GROUP:2 * GROUP] = (col(2) * acc).astype(o_ref.dtype)

    lane = lax.broadcasted_iota(I32, (1, GROUP), 1)
    rr = lax.broadcasted_iota(I32, (SGU_CHUNK, SGU_CHUNK), 0)
    cc = lax.broadcasted_iota(I32, (SGU_CHUNK, SGU_CHUNK), 1)
    hd = GROUP // SGU_HEADS
    w_tril = [jnp.where(rr >= cc, ws_ref[h], 0.0).astype(BF16) for h in range(SGU_HEADS)]
    for c in range(ts // SGU_CHUNK):
        rows = slice(c * SGU_CHUNK, (c + 1) * SGU_CHUNK)
        u = _gelu(cur_ref[rows, 5 * GROUP:6 * GROUP])
        v = _layer_norm(_gelu(cur_ref[rows, 6 * GROUP:7 * GROUP]), sg_ref[...], sb_ref[...]).astype(BF16)
        z = bs_ref[...]
        for h in range(SGU_HEADS):
            zh_ = jnp.dot(w_tril[h], v, preferred_element_type=F32)
            z = z + jnp.where((lane >= h * hd) & (lane < (h + 1) * hd), zh_, 0.0)
        o_ref[rows, 2 * GROUP:3 * GROUP] = (u * z).astype(o_ref.dtype)


def _conv_call(pc, conf_dw, conf_ln_g, conf_ln_b, sconv_w, sgu_ln_g, sgu_ln_b, sgu_w, sgu_bias_tile,
               batch, seq):
    t, n = pc.shape
    ts = 512
    nst = seq // ts

    def halo_map(rows, colblk):
        per = ts // rows
        return lambda b, i: (jnp.maximum((b * nst + i) * per - 1, 0), colblk)

    vec = lambda: pl.BlockSpec((1, GROUP), lambda b, i: (0, 0))
    return pl.pallas_call(
        _conv_kernel,
        grid=(batch, nst),
        in_specs=[pl.BlockSpec((ts, n), lambda b, i: (b * nst + i, 0)),
                  pl.BlockSpec((CONV_HALO, GROUP), halo_map(CONV_HALO, 0)),
                  pl.BlockSpec((CONV_HALO, GROUP), halo_map(CONV_HALO, 1)),
                  pl.BlockSpec((SCONV_HALO, GROUP), halo_map(SCONV_HALO, 3)),
                  pl.BlockSpec((SCONV_HALO, GROUP), halo_map(SCONV_HALO, 4)),
                  pl.BlockSpec((CONF_W, GROUP), lambda b, i: (0, 0)),
                  vec(), vec(),
                  pl.BlockSpec((SCONV_W, GROUP), lambda b, i: (0, 0)),
                  vec(), vec(),
                  pl.BlockSpec((SGU_HEADS, SGU_CHUNK, SGU_CHUNK), lambda b, i: (0, 0, 0)),
                  pl.BlockSpec((SGU_CHUNK, GROUP), lambda b, i: (0, 0))],
        out_specs=pl.BlockSpec((ts, 3 * GROUP), lambda b, i: (b * nst + i, 0)),
        out_shape=jax.ShapeDtypeStruct((t, 3 * GROUP), BF16),
        scratch_shapes=[pltpu.VMEM((CONV_HALO + ts, GROUP), F32),
                        pltpu.VMEM((SCONV_HALO + ts, GROUP), F32)],
        compiler_params=_cparams("arbitrary", "arbitrary"),
        name="conv_mix",
    )(pc, pc, pc, pc, pc, conf_dw, conf_ln_g, conf_ln_b, sconv_w, sgu_ln_g, sgu_ln_b, sgu_w,
      sgu_bias_tile)


def _extract_top(s, rowi, n_rounds, want_rank):
    sentinel = float(s.shape[0])
    vals = []
    mark = jnp.full(s.shape, float(n_rounds) if want_rank else 0.0, F32)
    for r in range(n_rounds):
        m = jnp.max(s, axis=0, keepdims=True)
        idx = jnp.min(jnp.where(s == m, rowi, sentinel), axis=0, keepdims=True)
        hit = rowi == idx
        mark = jnp.where(hit, float(r) if want_rank else 1.0, mark)
        s = jnp.where(hit, -jnp.inf, s)
        vals.append(m)
    return vals, mark


def _pack_pair(x):
    bits = pltpu.bitcast(x.astype(BF16).astype(F32), jnp.uint32)
    return bits | (bits >> 16)


def _outtop_kernel(x_ref, yat_ref, yb_ref, mod_ref, wo_ref, g_ref, wq_ref, keys_ref,
                   xo_ref, h2t_ref, c1_ref, e1_ref, r2_ref, e2_ref, qs_ref):
    d = x_ref.shape[1]
    tm = x_ref.shape[0]
    y = lax.dot_general(yat_ref[...], wo_ref[0:GROUP, :], _TN, preferred_element_type=F32)
    y = y + jnp.dot(yb_ref[...], wo_ref[GROUP:, :], preferred_element_type=F32)
    x = x_ref[...] + mod_ref[0, :, 2 * d:3 * d] * y
    xo_ref[...] = x
    h2f = _modulated_norm(x, g_ref[...], mod_ref[0, :, 3 * d:4 * d], mod_ref[0, :, 4 * d:5 * d])
    h2 = h2f.astype(BF16)
    h2t_ref[...] = jnp.transpose(h2f).astype(BF16)
    q = jnp.dot(h2, wq_ref[...], preferred_element_type=F32).astype(BF16)
    for hp in range(2 * PEER_HEADS):
        qs_ref[hp] = q[:, hp * NKEYS:(hp + 1) * NKEYS]

    rowi = lax.broadcasted_iota(I32, (NKEYS, tm), 0).astype(F32)
    rowc = lax.broadcasted_iota(I32, (N_CAND_PAD, tm), 0)
    rowc_f = rowc.astype(F32)
    a_row = jnp.zeros((N_CAND_PAD, tm), I32) - 1
    for st in CAND_STARTS:
        a_row = a_row + (rowc >= st).astype(I32)
    a_row = jnp.where(rowc >= N_CAND, TOPK, a_row)
    start_row = jnp.zeros((N_CAND_PAD, tm), I32)
    for a, st in enumerate(CAND_STARTS):
        start_row = jnp.where(a_row == a, st, start_row)
    b_row = rowc - start_row

    def head(hh, carry):
        halves = []
        for p in range(2):
            s = lax.dot_general(keys_ref[2 * hh + p], qs_ref[2 * hh + p], _NT,
                                preferred_element_type=F32)
            vals, rank = _extract_top(s, rowi, TOPK, True)
            halves.append((s, vals, rank))
        (s1, v1, r1), (s2, v2, r2) = halves
        rep1 = jnp.full((N_CAND_PAD, tm), -jnp.inf, F32)
        rep2 = jnp.full((N_CAND_PAD, tm), -jnp.inf, F32)
        for a in range(TOPK):
            rep1 = jnp.where(a_row == a, v1[a], rep1)
            rep2 = jnp.where(b_row == a, v2[a], rep2)
        cand = rep1 + rep2
        _, sel = _extract_top(cand, rowc_f, TOPK, False)
        cmax = v1[0] + v2[0]
        z = jnp.sum(sel * jnp.exp(jnp.where(sel > 0.0, cand, cmax) - cmax), axis=0, keepdims=True)
        c1 = jnp.zeros((NKEYS, tm), F32)
        for a in range(TOPK):
            cnt = jnp.sum(jnp.where(a_row == a, sel, 0.0), axis=0, keepdims=True)
            c1 = jnp.where(r1 == float(a), cnt, c1)
        c1_ref[hh] = _pack_pair(c1)
        e1_ref[hh] = _pack_pair(jnp.exp(s1 - v1[0]) / z)
        r2_ref[hh] = r2.astype(BF16)
        e2_ref[hh] = jnp.exp(s2 - v2[0]).astype(BF16)
        return carry

    lax.fori_loop(0, PEER_HEADS, head, 0)


def _outtop_call(x, yat, ybcd, mod, w_out, g2, wq, keys, seq):
    t, d = x.shape
    tm = 256
    nq = wq.shape[1]
    tab = lambda: pl.BlockSpec((PEER_HEADS, NKEYS, tm), lambda i: (0, 0, i))
    pair_shape = jax.ShapeDtypeStruct((PEER_HEADS, NKEYS, t), jnp.uint32)
    half_shape = jax.ShapeDtypeStruct((PEER_HEADS, NKEYS, t), BF16)
    return pl.pallas_call(
        _outtop_kernel,
        grid=(t // tm,),
        in_specs=[pl.BlockSpec((tm, d), lambda i: (i, 0)),
                  pl.BlockSpec((GROUP, tm), lambda i: (0, i)),
                  pl.BlockSpec((tm, 3 * GROUP), lambda i: (i, 0)),
                  pl.BlockSpec((1, 1, N_MOD * d), lambda i: ((i * tm) // seq, 0, 0)),
                  pl.BlockSpec((4 * GROUP, d), lambda i: (0, 0)),
                  pl.BlockSpec((1, d), lambda i: (0, 0)),
                  pl.BlockSpec((d, nq), lambda i: (0, 0)),
                  pl.BlockSpec((2 * PEER_HEADS, NKEYS, NKEYS), lambda i: (0, 0, 0))],
        out_specs=[pl.BlockSpec((tm, d), lambda i: (i, 0)),
                   pl.BlockSpec((d, tm), lambda i: (0, i)),
                   tab(), tab(), tab(), tab()],
        out_shape=[jax.ShapeDtypeStruct((t, d), F32),
                   jax.ShapeDtypeStruct((d, t), BF16),
                   pair_shape, pair_shape, half_shape, half_shape],
        scratch_shapes=[pltpu.VMEM((2 * PEER_HEADS, tm, NKEYS), BF16)],
        compiler_params=_cparams("arbitrary"),
        name="out_topk",
    )(x, yat, ybcd, mod, w_out, g2, wq, keys)


def _bcast_pair_rows(row_u32, n_rows):
    tile = pltpu.bitcast(jnp.broadcast_to(row_u32, (SUBLANES, row_u32.shape[1])), BF16)
    return jnp.concatenate([tile] * (n_rows // (2 * SUBLANES)), axis=0)


def _peer_kernel(h2t_ref, u_ref, x_ref, mod_ref, c1_ref, e1_ref, r2_ref, e2_ref, vt_ref, fg_ref,
                 o_ref, acc_ref, a0_ref, a1_ref, *, nk, final):
    s = pl.program_id(0)
    kp = jnp.maximum(s - 1, 0) % nk
    nb = u_ref.shape[0] // NKEYS
    d = x_ref.shape[1]

    @pl.when(s == 0)
    def _():
        a1_ref[...] = jnp.zeros_like(a1_ref)

    @pl.when(kp == 0)
    def _():
        acc_ref[...] = jnp.zeros_like(acc_ref)

    n_col = x_ref.shape[0] // PEER_TN

    def body(a_cur, a_prev):
        def cols(j):
            return slice(j * PEER_TN, (j + 1) * PEER_TN)

        def first_matmul(j):
            a_cur[:, cols(j)] = jnp.dot(u_ref[...], h2t_ref[:, cols(j)], preferred_element_type=F32)

        def gated_act(j):
            blocks = []
            for ii in range(nb):
                act = _gelu(a_prev[ii * NKEYS:(ii + 1) * NKEYS, cols(j)]).astype(BF16)
                gate = None
                for hh in range(PEER_HEADS):
                    c1 = _bcast_pair_rows(c1_ref[hh, kp, ii:ii + 1, cols(j)], NKEYS)
                    e1 = _bcast_pair_rows(e1_ref[hh, kp, ii:ii + 1, cols(j)], NKEYS)
                    g = jnp.where(r2_ref[hh, :, cols(j)] < c1, e2_ref[hh, :, cols(j)] * e1,
                                  jnp.zeros_like(e1))
                    gate = g if gate is None else gate + g
                blocks.append(gate * act)
            return jnp.concatenate(blocks, axis=0)

        def second_matmul(j, w):
            acc_ref[:, cols(j)] += jnp.dot(vt_ref[...], w, preferred_element_type=F32)

        first_matmul(0)
        for j in range(n_col):
            if j + 1 < n_col:
                first_matmul(j + 1)
            second_matmul(j, gated_act(j))

    @pl.when(s % 2 == 0)
    def _():
        body(a0_ref, a1_ref)

    @pl.when(s % 2 == 1)
    def _():
        body(a1_ref, a0_ref)

    @pl.when((kp == nk - 1) & (s >= 1))
    def _():
        y = jnp.transpose(acc_ref[...])
        x = x_ref[...] + mod_ref[0, :, 5 * d:6 * d] * y
        if final:
            ms = jnp.mean(x * x, axis=-1, keepdims=True)
            x = x * lax.rsqrt(ms + EPS) * fg_ref[...]
        o_ref[...] = x


def _peer_call(x, h2, mod, c1, e1, r2, e2, u, vt, final_g, seq, final):
    t, d = x.shape
    nk, _, ec = vt.shape
    tm = PEER_TM
    n_i = t // tm
    last = n_i * nk - 1
    cur_i = lambda s: jnp.minimum(s, last) // nk
    cur_k = lambda s: jnp.minimum(s, last) % nk
    prev_i = lambda s: jnp.maximum(s - 1, 0) // nk
    prev_k = lambda s: jnp.maximum(s - 1, 0) % nk
    once = pl.Buffered(1)
    tab = lambda: pl.BlockSpec((PEER_HEADS, NKEYS, tm), lambda s: (0, 0, prev_i(s)), pipeline_mode=once)
    rows = lambda: pl.BlockSpec((PEER_HEADS, nk, NKEYS // nk, tm), lambda s: (0, 0, 0, prev_i(s)),
                                pipeline_mode=once)
    c1 = c1.reshape(PEER_HEADS, nk, NKEYS // nk, t)
    e1 = e1.reshape(PEER_HEADS, nk, NKEYS // nk, t)
    return pl.pallas_call(
        functools.partial(_peer_kernel, nk=nk, final=final),
        grid=(n_i * nk + 1,),
        in_specs=[pl.BlockSpec((d, tm), lambda s: (0, cur_i(s))),
                  pl.BlockSpec((ec, d), lambda s: (cur_k(s), 0)),
                  pl.BlockSpec((tm, d), lambda s: (prev_i(s), 0), pipeline_mode=once),
                  pl.BlockSpec((1, 1, N_MOD * d), lambda s: ((prev_i(s) * tm) // seq, 0, 0)),
                  rows(), rows(), tab(), tab(),
                  pl.BlockSpec((None, d, ec), lambda s: (prev_k(s), 0, 0)),
                  pl.BlockSpec((1, d), lambda s: (0, 0))],
        out_specs=pl.BlockSpec((tm, d), lambda s: (prev_i(s), 0), pipeline_mode=once),
        out_shape=jax.ShapeDtypeStruct((t, d), F32),
        scratch_shapes=[pltpu.VMEM((d, tm), F32),
                        pltpu.VMEM((ec, tm), F32),
                        pltpu.VMEM((ec, tm), F32)],
        compiler_params=_cparams("arbitrary"),
        name="peer_dense",
    )(h2, u, x, mod, c1, e1, r2, e2, vt, final_g)


def kernel(x, c, rel_bias, w_mod, b_mod, norm1_g, norm2_g, w_in, w_out, diff_lambda, subln_g, conf_dw, conf_ln_g, conf_ln_b, sconv_w, sgu_ln_g, sgu_ln_b, sgu_w, sgu_b, peer_wq, peer_keys, peer_u, peer_v, final_g):
    batch, seq, d = x.shape
    depth = w_in.shape[0]
    t = batch * seq
    assert w_in.shape[2] == N_SLICES * GROUP and seq % PEER_TM == 0 and d % LANES == 0

    xf = x.reshape(t, d)
    mods = _mod_call(c, w_mod, b_mod).reshape(depth, batch, 1, N_MOD * d)
    attn_bias = _attn_bias_tiles(rel_bias)
    row = lambda v: v.reshape(1, -1)

    for l in range(depth):
        lam_init = 0.8 - 0.6 * math.exp(-0.3 * l)
        wl = w_in[l]
        wqv_t = jnp.transpose(jnp.concatenate([wl[:, 0:GROUP], wl[:, 2 * GROUP:3 * GROUP]], axis=1)).astype(BF16)
        w_rest = jnp.concatenate([wl[:, GROUP:2 * GROUP], wl[:, 3 * GROUP:]], axis=1).astype(BF16)
        qt, k, vt, pc = _in_call(xf, mods[l], row(norm1_g[l]), wqv_t, w_rest, seq)
        yat = _attn_call(qt, k, vt, attn_bias, diff_lambda[l], subln_g[l].reshape(ATT_V, 1),
                         batch, seq, lam_init)
        sgu_bias_tile = jnp.repeat(jnp.transpose(sgu_b[l]), GROUP // SGU_HEADS, axis=1)
        ybcd = _conv_call(pc, conf_dw[l], row(conf_ln_g[l]), row(conf_ln_b[l]), sconv_w[l],
                          row(sgu_ln_g[l]), row(sgu_ln_b[l]), sgu_w[l], sgu_bias_tile, batch, seq)
        keys = peer_keys[l].reshape(2 * PEER_HEADS, NKEYS, -1).astype(BF16)
        xf, h2, c1, e1, r2, e2 = _outtop_call(xf, yat, ybcd, mods[l], w_out[l].astype(BF16),
                                              row(norm2_g[l]), peer_wq[l].astype(BF16), keys, seq)
        vt = jnp.transpose(peer_v[l].reshape(-1, PEER_EC, d), (0, 2, 1)).astype(BF16)
        xf = _peer_call(xf, h2, mods[l], c1, e1, r2, e2, peer_u[l].astype(BF16), vt, row(final_g),
                        seq, l == depth - 1)
    return xf.reshape(batch, seq, d)
```

```python
import functools
import math

import jax
import jax.numpy as jnp
import numpy as np
from jax import lax
from jax.experimental import pallas as pl
from jax.experimental.pallas import tpu as pltpu

F32 = jnp.float32
BF16 = jnp.bfloat16
I32 = jnp.int32
HIGHEST = lax.Precision.HIGHEST

GROUP = 256
N_SLICES = 10
ATT_HEADS = 4
ATT_V = 64
ATT_QK = 32
N_MAPS = 2 * ATT_HEADS
NUM_BUCKETS = 32
MAX_DISTANCE = 128
CONF_W = 31
SCONV_W = 3
SGU_HEADS = 4
SGU_CHUNK = 128
PEER_HEADS = 8
NKEYS = 128
TOPK = 16
N_MOD = 6
EPS = 1e-6
NEG_INF = -1e30
SQRT_HALF = float(np.sqrt(0.5))

LANES = 128
SUBLANES = 8
VMEM_LIMIT_BYTES = 56 * 1024 * 1024

PEER_TM = 1024
PEER_EC = 1024
PEER_TN = 256
ATQ = 256
ONES_ROWS = 16

CAND_COUNTS = [TOPK // (a + 1) for a in range(TOPK)]
CAND_STARTS = [int(v) for v in np.cumsum([0] + CAND_COUNTS[:-1])]
N_CAND = sum(CAND_COUNTS)
N_CAND_PAD = -(-N_CAND // SUBLANES) * SUBLANES

_NT = (((1,), (1,)), ((), ()))
_TN = (((0,), (0,)), ((), ()))


def _gelu(x):
    return 0.5 * x * (1.0 + lax.erf(x * SQRT_HALF))


def _cparams(*sem):
    return pltpu.CompilerParams(dimension_semantics=sem, vmem_limit_bytes=VMEM_LIMIT_BYTES)


def _mod_kernel(c_ref, w_ref, b_ref, o_ref):
    c = c_ref[...]
    cond = c * jax.nn.sigmoid(c)
    o_ref[0] = jnp.dot(cond, w_ref[0], precision=HIGHEST, preferred_element_type=F32) + b_ref[0]


def _mod_call(c, w_mod, b_mod):
    depth, d, n = w_mod.shape
    b = c.shape[0]
    tn = 2048
    return pl.pallas_call(
        _mod_kernel,
        grid=(depth, n // tn),
        in_specs=[pl.BlockSpec((b, d), lambda l, j: (0, 0)),
                  pl.BlockSpec((1, d, tn), lambda l, j: (l, 0, j)),
                  pl.BlockSpec((1, 1, tn), lambda l, j: (l, 0, j))],
        out_specs=pl.BlockSpec((1, b, tn), lambda l, j: (l, 0, j)),
        out_shape=jax.ShapeDtypeStruct((depth, b, n), F32),
        compiler_params=_cparams("arbitrary", "arbitrary"),
        name="mod",
    )(c, w_mod, b_mod.reshape(depth, 1, n))


def _modulated_norm(x, g, shift, scale):
    ms = jnp.mean(x * x, axis=-1, keepdims=True)
    return (x * lax.rsqrt(ms + EPS) * g) * (1.0 + scale) + shift


def _in_kernel(x_ref, mod_ref, g_ref, wqv_ref, wr_ref, qt_ref, k_ref, vt_ref, pc_ref):
    d = x_ref.shape[1]
    h = _modulated_norm(x_ref[...], g_ref[...], mod_ref[0, :, 0:d], mod_ref[0, :, d:2 * d]).astype(BF16)
    qv = lax.dot_general(wqv_ref[...], h, _NT, preferred_element_type=F32)
    qt_ref[...] = (qv[0:GROUP] * (ATT_QK ** -0.5)).astype(BF16)
    for c in range(vt_ref.shape[0]):
        vt_ref[c] = qv[GROUP:2 * GROUP, c * ATQ:(c + 1) * ATQ].astype(BF16)
    res = jnp.dot(h, wr_ref[...], preferred_element_type=F32)
    k_ref[...] = res[:, 0:GROUP].astype(BF16)
    pc_ref[...] = res[:, GROUP:]


def _in_call(x, mod, g, wqv_t, w_rest, seq):
    t, d = x.shape
    n = w_rest.shape[1]
    tm = 512
    return pl.pallas_call(
        _in_kernel,
        grid=(t // tm,),
        in_specs=[pl.BlockSpec((tm, d), lambda i: (i, 0)),
                  pl.BlockSpec((1, 1, N_MOD * d), lambda i: ((i * tm) // seq, 0, 0)),
                  pl.BlockSpec((1, d), lambda i: (0, 0)),
                  pl.BlockSpec((2 * GROUP, d), lambda i: (0, 0)),
                  pl.BlockSpec((d, n), lambda i: (0, 0))],
        out_specs=[pl.BlockSpec((GROUP, tm), lambda i: (0, i)),
                   pl.BlockSpec((tm, GROUP), lambda i: (i, 0)),
                   pl.BlockSpec((tm // ATQ, GROUP, ATQ), lambda i: (i, 0, 0)),
                   pl.BlockSpec((tm, n - GROUP), lambda i: (i, 0))],
        out_shape=[jax.ShapeDtypeStruct((GROUP, t), BF16),
                   jax.ShapeDtypeStruct((t, GROUP), BF16),
                   jax.ShapeDtypeStruct((t // ATQ, GROUP, ATQ), BF16),
                   jax.ShapeDtypeStruct((t, n - GROUP), F32)],
        compiler_params=_cparams("arbitrary"),
        name="in_proj",
    )(x, mod, g, wqv_t, w_rest)


def _attn_kernel(qt_ref, k_ref, vt_ref, bias_ref, lam_ref, g_ref, o_ref, qz_ref, acc_ref, m_ref, *, lam_init):
    i = pl.program_id(1)
    qt = qt_ref[...]
    rowi = lax.broadcasted_iota(I32, (GROUP, ATQ), 0)
    for c in range(N_MAPS):
        lo = c * ATT_QK
        qz_ref[c] = jnp.where((rowi >= lo) & (rowi < lo + ATT_QK), qt, jnp.zeros_like(qt))
    ones = jnp.ones((ONES_ROWS, ATQ), BF16)

    def step(jb, bias_idx, init):
        kb = k_ref[pl.ds(pl.multiple_of(jb * ATQ, ATQ), ATQ), :]
        vb = vt_ref[jb]
        m_old = None if init else m_ref[...]
        acc_old = None if init else [acc_ref[c] for c in range(N_MAPS)]
        scores = [jnp.dot(kb, qz_ref[c], preferred_element_type=F32) for c in range(N_MAPS)]
        m_new, probs, alphas = [], [], []
        for c in range(N_MAPS):
            s = scores[c]
            if bias_idx is not None:
                s = s + bias_ref[c // 2, bias_idx]
            mn = jnp.max(s, axis=0, keepdims=True)
            if not init:
                mo = m_old[c:c + 1, :]
                mn = jnp.maximum(mo, mn)
                alphas.append(jnp.exp(mo - mn))
            probs.append(jnp.exp(s - mn).astype(BF16))
            m_new.append(mn)
        acc_new = []
        for c in range(N_MAPS):
            h = c // 2
            vaug = jnp.concatenate([vb[h * ATT_V:(h + 1) * ATT_V, :], ones], axis=0)
            pv = jnp.dot(vaug, probs[c], preferred_element_type=F32)
            acc_new.append(pv if init else alphas[c] * acc_old[c] + pv)
        m_ref[...] = jnp.concatenate(m_new, axis=0)
        for c in range(N_MAPS):
            acc_ref[c] = acc_new[c]

    step(i, 1, True)

    @pl.when(i >= 1)
    def _():
        step(i - 1, 0, False)

    def far(j, carry):
        step(j, None, False)
        return carry

    lax.fori_loop(0, jnp.maximum(i - 1, 0), far, 0)

    lp = lam_ref[...]
    lam = (jnp.exp(jnp.sum(lp[0:1] * lp[1:2], axis=1, keepdims=True))
           - jnp.exp(jnp.sum(lp[2:3] * lp[3:4], axis=1, keepdims=True)) + lam_init)
    for h in range(ATT_HEADS):
        a0 = acc_ref[2 * h]
        a1 = acc_ref[2 * h + 1]
        o = a0[0:ATT_V] / a0[ATT_V:ATT_V + 1] - lam * (a1[0:ATT_V] / a1[ATT_V:ATT_V + 1])
        ms = jnp.mean(o * o, axis=0, keepdims=True)
        y = o * lax.rsqrt(ms + EPS) * g_ref[...] * (1.0 - lam_init)
        o_ref[h * ATT_V:(h + 1) * ATT_V, :] = y.astype(o_ref.dtype)


def _attn_call(qt, k, vt, bias, lam_par, subln_col, batch, seq, lam_init):
    t = k.shape[0]
    nq = seq // ATQ
    return pl.pallas_call(
        functools.partial(_attn_kernel, lam_init=lam_init),
        grid=(batch, nq),
        in_specs=[pl.BlockSpec((GROUP, ATQ), lambda b, i: (0, b * nq + i)),
                  pl.BlockSpec((seq, GROUP), lambda b, i: (b, 0)),
                  pl.BlockSpec((nq, GROUP, ATQ), lambda b, i: (b, 0, 0)),
                  pl.BlockSpec((ATT_HEADS, 2, ATQ, ATQ), lambda b, i: (0, 0, 0, 0)),
                  pl.BlockSpec((4, ATT_QK), lambda b, i: (0, 0)),
                  pl.BlockSpec((ATT_V, 1), lambda b, i: (0, 0))],
        out_specs=pl.BlockSpec((GROUP, ATQ), lambda b, i: (0, b * nq + i)),
        out_shape=jax.ShapeDtypeStruct((GROUP, t), BF16),
        scratch_shapes=[pltpu.VMEM((N_MAPS, GROUP, ATQ), BF16),
                        pltpu.VMEM((N_MAPS, ATT_V + ONES_ROWS, ATQ), F32),
                        pltpu.VMEM((N_MAPS, ATQ), F32)],
        compiler_params=_cparams("arbitrary", "arbitrary"),
        name="diff_attn",
    )(qt, k, vt, bias, lam_par, subln_col)


def _rel_bucket_np(n):
    max_exact = NUM_BUCKETS // 2
    n = np.maximum(n, 0)
    ratio = (np.log(np.maximum(n, 1).astype(np.float32) / np.float32(max_exact))
             / np.float32(math.log(MAX_DISTANCE / max_exact)))
    large = np.minimum(max_exact + (ratio * (NUM_BUCKETS - max_exact)).astype(np.int32), NUM_BUCKETS - 1)
    return np.where(n < max_exact, n, large).astype(np.int32)


def _bias_kernel(bucket_ref, rb_ref, o_ref):
    h = pl.program_id(0)
    far = rb_ref[NUM_BUCKETS - 1, h]
    for tile in range(2):
        b = bucket_ref[tile]
        acc = jnp.full(b.shape, NEG_INF, F32)
        for k in range(NUM_BUCKETS):
            acc = jnp.where(b == k, rb_ref[k, h] - far, acc)
        o_ref[0, tile] = acc


def _attn_bias_tiles(rel_bias):
    rel_cur = np.arange(ATQ)[None, :] - np.arange(ATQ)[:, None]
    assert np.all(_rel_bucket_np(np.arange(ATQ + 1, 8 * ATQ)) == NUM_BUCKETS - 1)
    buckets = np.stack([_rel_bucket_np(rel_cur + ATQ),
                        np.where(rel_cur >= 0, _rel_bucket_np(rel_cur), NUM_BUCKETS)]).astype(np.int32)
    return pl.pallas_call(
        _bias_kernel,
        grid=(ATT_HEADS,),
        in_specs=[pl.BlockSpec((2, ATQ, ATQ), lambda h: (0, 0, 0)),
                  pl.BlockSpec(memory_space=pltpu.SMEM)],
        out_specs=pl.BlockSpec((1, 2, ATQ, ATQ), lambda h: (h, 0, 0, 0)),
        out_shape=jax.ShapeDtypeStruct((ATT_HEADS, 2, ATQ, ATQ), F32),
        compiler_params=_cparams("arbitrary"),
        name="attn_bias",
    )(jnp.asarray(buckets), rel_bias)


CONV_HALO = 32
SCONV_HALO = 8


def _layer_norm(x, g, b):
    mu = jnp.mean(x, axis=-1, keepdims=True)
    xc = x - mu
    var = jnp.mean(xc * xc, axis=-1, keepdims=True)
    return xc * lax.rsqrt(var + EPS) * g + b


def _conv_kernel(cur_ref, hga_ref, hgb_ref, hcc_ref, hch_ref, dw_ref, lng_ref, lnb_ref, sw_ref,
                 sg_ref, sb_ref, ws_ref, bs_ref, o_ref, ext_ref, ext2_ref):
    i = pl.program_id(1)
    ts = cur_ref.shape[0]
    first = i == 0

    def col(k):
        return cur_ref[:, k * GROUP:(k + 1) * GROUP]

    zh = hga_ref[...] * jax.nn.sigmoid(hgb_ref[...])
    ext_ref[0:CONV_HALO, :] = jnp.where(first, 0.0, zh)
    ext_ref[CONV_HALO:, :] = col(0) * jax.nn.sigmoid(col(1))
    acc = jnp.zeros((ts, GROUP), F32)
    for j in range(CONF_W):
        off = CONV_HALO - (CONF_W - 1) + j
        acc = acc + dw_ref[j:j + 1, :] * ext_ref[off:off + ts, :]
    zb = _layer_norm(acc, lng_ref[...], lnb_ref[...])
    o_ref[:, 0:GROUP] = (zb * jax.nn.sigmoid(zb)).astype(o_ref.dtype)

    ext2_ref[0:SCONV_HALO, :] = jnp.where(first, 0.0, hcc_ref[...] * hch_ref[...])
    ext2_ref[SCONV_HALO:, :] = col(3) * col(4)
    acc = jnp.zeros((ts, GROUP), F32)
    for j in range(SCONV_W):
        off = SCONV_HALO - (SCONV_W - 1) + j
        acc = acc + sw_ref[j:j + 1, :] * ext2_ref[off:off + ts, :]
    o_ref[:, GROUP:2 * GROUP] = (col(2) * acc).astype(o_ref.dtype)

    lane = lax.broadcasted_iota(I32, (1, GROUP), 1)
    rr = lax.broadcasted_iota(I32, (SGU_CHUNK, SGU_CHUNK), 0)
    cc = lax.broadcasted_iota(I32, (SGU_CHUNK, SGU_CHUNK), 1)
    hd = GROUP // SGU_HEADS
    w_tril = [jnp.where(rr >= cc, ws_ref[h], 0.0).astype(BF16) for h in range(SGU_HEADS)]
    for c in range(ts // SGU_CHUNK):
        rows = slice(c * SGU_CHUNK, (c + 1) * SGU_CHUNK)
        u = _gelu(cur_ref[rows, 5 * GROUP:6 * GROUP])
        v = _layer_norm(_gelu(cur_ref[rows, 6 * GROUP:7 * GROUP]), sg_ref[...], sb_ref[...]).astype(BF16)
        z = bs_ref[...]
        for h in range(SGU_HEADS):
            zh_ = jnp.dot(w_tril[h], v, preferred_element_type=F32)
            z = z + jnp.where((lane >= h * hd) & (lane < (h + 1) * hd), zh_, 0.0)
        o_ref[rows, 2 * GROUP:3 * GROUP] = (u * z).astype(o_ref.dtype)


def _conv_call(pc, conf_dw, conf_ln_g, conf_ln_b, sconv_w, sgu_ln_g, sgu_ln_b, sgu_w, sgu_bias_tile,
               batch, seq):
    t, n = pc.shape
    ts = 512
    nst = seq // ts

    def halo_map(rows, colblk):
        per = ts // rows
        return lambda b, i: (jnp.maximum((b * nst + i) * per - 1, 0), colblk)

    vec = lambda: pl.BlockSpec((1, GROUP), lambda b, i: (0, 0))
    return pl.pallas_call(
        _conv_kernel,
        grid=(batch, nst),
        in_specs=[pl.BlockSpec((ts, n), lambda b, i: (b * nst + i, 0)),
                  pl.BlockSpec((CONV_HALO, GROUP), halo_map(CONV_HALO, 0)),
                  pl.BlockSpec((CONV_HALO, GROUP), halo_map(CONV_HALO, 1)),
                  pl.BlockSpec((SCONV_HALO, GROUP), halo_map(SCONV_HALO, 3)),
                  pl.BlockSpec((SCONV_HALO, GROUP), halo_map(SCONV_HALO, 4)),
                  pl.BlockSpec((CONF_W, GROUP), lambda b, i: (0, 0)),
                  vec(), vec(),
                  pl.BlockSpec((SCONV_W, GROUP), lambda b, i: (0, 0)),
                  vec(), vec(),
                  pl.BlockSpec((SGU_HEADS, SGU_CHUNK, SGU_CHUNK), lambda b, i: (0, 0, 0)),
                  pl.BlockSpec((SGU_CHUNK, GROUP), lambda b, i: (0, 0))],
        out_specs=pl.BlockSpec((ts, 3 * GROUP), lambda b, i: (b * nst + i, 0)),
        out_shape=jax.ShapeDtypeStruct((t, 3 * GROUP), BF16),
        scratch_shapes=[pltpu.VMEM((CONV_HALO + ts, GROUP), F32),
                        pltpu.VMEM((SCONV_HALO + ts, GROUP), F32)],
        compiler_params=_cparams("arbitrary", "arbitrary"),
        name="conv_mix",
    )(pc, pc, pc, pc, pc, conf_dw, conf_ln_g, conf_ln_b, sconv_w, sgu_ln_g, sgu_ln_b, sgu_w,
      sgu_bias_tile)


def _extract_top(s, rowi, n_rounds, want_rank):
    sentinel = float(s.shape[0])
    vals = []
    mark = jnp.full(s.shape, float(n_rounds) if want_rank else 0.0, F32)
    for r in range(n_rounds):
        m = jnp.max(s, axis=0, keepdims=True)
        idx = jnp.min(jnp.where(s == m, rowi, sentinel), axis=0, keepdims=True)
        hit = rowi == idx
        mark = jnp.where(hit, float(r) if want_rank else 1.0, mark)
        s = jnp.where(hit, -jnp.inf, s)
        vals.append(m)
    return vals, mark


def _batcher_network(n):
    pairs = []

    def merge(lo, m, r):
        step = 2 * r
        if step < m:
            merge(lo, m, step)
            merge(lo + r, m, step)
            pairs.extend((i, i + r) for i in range(lo + r, lo + m - r, step))
        else:
            pairs.append((lo, lo + r))

    def sort(lo, m):
        if m > 1:
            sort(lo, m // 2)
            sort(lo + m // 2, m // 2)
            merge(lo, m, 1)

    sort(0, n)
    return pairs


def _compare_exchange(blocks, i, j):
    hi, lo = jnp.maximum(blocks[i], blocks[j]), jnp.minimum(blocks[i], blocks[j])
    blocks[i], blocks[j] = hi, lo


def _bitonic_merge(blocks):
    n = len(blocks)
    d = n // 2
    while d >= 1:
        for i in range(n):
            if i & d == 0:
                _compare_exchange(blocks, i, i + d)
        d //= 2
    return blocks


def _sorted_top(s):
    n = s.shape[0] // SUBLANES
    assert n == TOPK
    blocks = [s[j * SUBLANES:(j + 1) * SUBLANES, :] for j in range(n)]
    for i, j in _batcher_network(n):
        _compare_exchange(blocks, i, j)
    shift = SUBLANES // 2
    while shift >= 1:
        rolled = [pltpu.roll(b, shift, 0) for b in blocks]
        blocks = _bitonic_merge([jnp.maximum(blocks[j], rolled[n - 1 - j]) for j in range(n)])
        shift //= 2
    return blocks


def _extract_values(s, n_rounds):
    vals = []
    for _ in range(n_rounds):
        m = jnp.max(s, axis=0, keepdims=True)
        s = jnp.where(s == m, -jnp.inf, s)
        vals.append(m)
    return vals


def _tile_rows(block, n_rows):
    return jnp.concatenate([block] * (n_rows // block.shape[0]), axis=0)


def _pack_pair(x):
    bits = pltpu.bitcast(x.astype(BF16).astype(F32), jnp.uint32)
    return bits | (bits >> 16)


def _outtop_kernel(x_ref, yat_ref, yb_ref, mod_ref, wo_ref, g_ref, wq_ref, keys_ref,
                   xo_ref, h2t_ref, c1_ref, e1_ref, r2_ref, e2_ref, qs_ref):
    d = x_ref.shape[1]
    tm = x_ref.shape[0]
    y = lax.dot_general(yat_ref[...], wo_ref[0:GROUP, :], _TN, preferred_element_type=F32)
    y = y + jnp.dot(yb_ref[...], wo_ref[GROUP:, :], preferred_element_type=F32)
    x = x_ref[...] + mod_ref[0, :, 2 * d:3 * d] * y
    xo_ref[...] = x
    h2f = _modulated_norm(x, g_ref[...], mod_ref[0, :, 3 * d:4 * d], mod_ref[0, :, 4 * d:5 * d])
    h2 = h2f.astype(BF16)
    h2t_ref[...] = jnp.transpose(h2f).astype(BF16)
    q = jnp.dot(h2, wq_ref[...], preferred_element_type=F32).astype(BF16)
    for hp in range(2 * PEER_HEADS):
        qs_ref[hp] = q[:, hp * NKEYS:(hp + 1) * NKEYS]

    rowi = lax.broadcasted_iota(I32, (NKEYS, tm), 0).astype(F32)
    rowc = lax.broadcasted_iota(I32, (N_CAND_PAD, tm), 0)
    rowc_f = rowc.astype(F32)
    a_row = jnp.zeros((N_CAND_PAD, tm), I32) - 1
    for st in CAND_STARTS:
        a_row = a_row + (rowc >= st).astype(I32)
    a_row = jnp.where(rowc >= N_CAND, TOPK, a_row)
    start_row = jnp.zeros((N_CAND_PAD, tm), I32)
    for a, st in enumerate(CAND_STARTS):
        start_row = jnp.where(a_row == a, st, start_row)
    b_row = rowc - start_row

    def candidates(v1, v2):
        rep1 = jnp.full((N_CAND_PAD, tm), -jnp.inf, F32)
        rep2 = jnp.full((N_CAND_PAD, tm), -jnp.inf, F32)
        for a in range(TOPK):
            rep1 = jnp.where(a_row == a, v1[a], rep1)
            rep2 = jnp.where(b_row == a, v2[a], rep2)
        return rep1 + rep2

    def group_counts(sel):
        return [jnp.sum(sel[st:st + n], axis=0, keepdims=True) for st, n in zip(CAND_STARTS, CAND_COUNTS)]

    def store(hh, s1, s2, max1, max2, z, c1, r2):
        c1_ref[hh] = _pack_pair(c1)
        e1_ref[hh] = _pack_pair(jnp.exp(s1 - max1) / z)
        r2_ref[hh] = r2.astype(BF16)
        e2_ref[hh] = jnp.exp(s2 - max2).astype(BF16)

    def head(hh, carry):
        s1, s2 = [lax.dot_general(keys_ref[2 * hh + p], qs_ref[2 * hh + p], _NT,
                                  preferred_element_type=F32) for p in range(2)]

        v1 = _sorted_top(s1)
        v2 = _sorted_top(s2)
        cand = candidates([v[0:1] for v in v1], [v[0:1] for v in v2])
        top = _extract_values(cand, TOPK)
        cmax, thr = top[0], top[TOPK - 1]
        z = jnp.zeros_like(cmax)
        for val in top:
            z = z + jnp.exp(val - cmax)
        counts = group_counts((cand >= thr).astype(F32))
        c1 = jnp.zeros((NKEYS, tm), F32)
        r2 = jnp.full((NKEYS, tm), float(TOPK), F32)
        for a in reversed(range(TOPK)):
            c1 = jnp.where(s1 == _tile_rows(v1[a], NKEYS), counts[a], c1)
            r2 = jnp.where(s2 == _tile_rows(v2[a], NKEYS), float(a), r2)
        store(hh, s1, s2, v1[0][0:1], v2[0][0:1], z, c1, r2)

        tied = jnp.sum(c1, axis=0, keepdims=True) != float(TOPK)
        for s, v in ((s1, v1), (s2, v2)):
            dup = v[0] == v[1]
            for a in range(1, TOPK - 1):
                dup = dup | (v[a] == v[a + 1])
            n_top = jnp.sum((s >= _tile_rows(v[TOPK - 1], NKEYS)).astype(F32), axis=0, keepdims=True)
            tied = tied | (n_top != float(TOPK)) | (jnp.max(dup.astype(F32), axis=0, keepdims=True) > 0.0)

        @pl.when(jnp.max(tied.astype(F32)) > 0.0)
        def _():
            w1, rank1 = _extract_top(s1, rowi, TOPK, True)
            w2, rank2 = _extract_top(s2, rowi, TOPK, True)
            cand_x = candidates(w1, w2)
            _, sel = _extract_top(cand_x, rowc_f, TOPK, False)
            cmax_x = w1[0] + w2[0]
            z_x = jnp.sum(sel * jnp.exp(jnp.where(sel > 0.0, cand_x, cmax_x) - cmax_x), axis=0, keepdims=True)
            counts_x = group_counts(sel)
            c1_x = jnp.zeros((NKEYS, tm), F32)
            for a in range(TOPK):
                c1_x = jnp.where(rank1 == float(a), counts_x[a], c1_x)
            store(hh, s1, s2, w1[0], w2[0], z_x, c1_x, rank2)

        return carry

    lax.fori_loop(0, PEER_HEADS, head, 0)


def _outtop_call(x, yat, ybcd, mod, w_out, g2, wq, keys, seq):
    t, d = x.shape
    tm = 256
    nq = wq.shape[1]
    tab = lambda: pl.BlockSpec((PEER_HEADS, NKEYS, tm), lambda i: (0, 0, i))
    pair_shape = jax.ShapeDtypeStruct((PEER_HEADS, NKEYS, t), jnp.uint32)
    half_shape = jax.ShapeDtypeStruct((PEER_HEADS, NKEYS, t), BF16)
    return pl.pallas_call(
        _outtop_kernel,
        grid=(t // tm,),
        in_specs=[pl.BlockSpec((tm, d), lambda i: (i, 0)),
                  pl.BlockSpec((GROUP, tm), lambda i: (0, i)),
                  pl.BlockSpec((tm, 3 * GROUP), lambda i: (i, 0)),
                  pl.BlockSpec((1, 1, N_MOD * d), lambda i: ((i * tm) // seq, 0, 0)),
                  pl.BlockSpec((4 * GROUP, d), lambda i: (0, 0)),
                  pl.BlockSpec((1, d), lambda i: (0, 0)),
                  pl.BlockSpec((d, nq), lambda i: (0, 0)),
                  pl.BlockSpec((2 * PEER_HEADS, NKEYS, NKEYS), lambda i: (0, 0, 0))],
        out_specs=[pl.BlockSpec((tm, d), lambda i: (i, 0)),
                   pl.BlockSpec((d, tm), lambda i: (0, i)),
                   tab(), tab(), tab(), tab()],
        out_shape=[jax.ShapeDtypeStruct((t, d), F32),
                   jax.ShapeDtypeStruct((d, t), BF16),
                   pair_shape, pair_shape, half_shape, half_shape],
        scratch_shapes=[pltpu.VMEM((2 * PEER_HEADS, tm, NKEYS), BF16)],
        compiler_params=_cparams("arbitrary"),
        name="out_topk",
    )(x, yat, ybcd, mod, w_out, g2, wq, keys)


def _bcast_pair_rows(row_u32, n_rows):
    tile = pltpu.bitcast(jnp.broadcast_to(row_u32, (SUBLANES, row_u32.shape[1])), BF16)
    return jnp.concatenate([tile] * (n_rows // (2 * SUBLANES)), axis=0)


def _peer_kernel(h2t_ref, u_ref, x_ref, mod_ref, c1_ref, e1_ref, r2_ref, e2_ref, vt_ref, fg_ref,
                 o_ref, acc_ref, a0_ref, a1_ref, *, nk, final):
    s = pl.program_id(0)
    kp = jnp.maximum(s - 1, 0) % nk
    nb = u_ref.shape[0] // NKEYS
    d = x_ref.shape[1]

    @pl.when(s == 0)
    def _():
        a1_ref[...] = jnp.zeros_like(a1_ref)

    @pl.when(kp == 0)
    def _():
        acc_ref[...] = jnp.zeros_like(acc_ref)

    n_col = x_ref.shape[0] // PEER_TN

    def body(a_cur, a_prev):
        def cols(j):
            return slice(j * PEER_TN, (j + 1) * PEER_TN)

        def first_matmul(j):
            a_cur[:, cols(j)] = jnp.dot(u_ref[...], h2t_ref[:, cols(j)], preferred_element_type=F32)

        def gated_act(j):
            blocks = []
            for ii in range(nb):
                act = _gelu(a_prev[ii * NKEYS:(ii + 1) * NKEYS, cols(j)]).astype(BF16)
                gate = None
                for hh in range(PEER_HEADS):
                    c1 = _bcast_pair_rows(c1_ref[hh, kp, ii:ii + 1, cols(j)], NKEYS)
                    e1 = _bcast_pair_rows(e1_ref[hh, kp, ii:ii + 1, cols(j)], NKEYS)
                    g = jnp.where(r2_ref[hh, :, cols(j)] < c1, e2_ref[hh, :, cols(j)] * e1,
                                  jnp.zeros_like(e1))
                    gate = g if gate is None else gate + g
                blocks.append(gate * act)
            return jnp.concatenate(blocks, axis=0)

        def second_matmul(j, w):
            acc_ref[:, cols(j)] += jnp.dot(vt_ref[...], w, preferred_element_type=F32)

        first_matmul(0)
        for j in range(n_col):
            if j + 1 < n_col:
                first_matmul(j + 1)
            second_matmul(j, gated_act(j))

    @pl.when(s % 2 == 0)
    def _():
        body(a0_ref, a1_ref)

    @pl.when(s % 2 == 1)
    def _():
        body(a1_ref, a0_ref)

    @pl.when((kp == nk - 1) & (s >= 1))
    def _():
        y = jnp.transpose(acc_ref[...])
        x = x_ref[...] + mod_ref[0, :, 5 * d:6 * d] * y
        if final:
            ms = jnp.mean(x * x, axis=-1, keepdims=True)
            x = x * lax.rsqrt(ms + EPS) * fg_ref[...]
        o_ref[...] = x


def _peer_call(x, h2, mod, c1, e1, r2, e2, u, vt, final_g, seq, final):
    t, d = x.shape
    nk, _, ec = vt.shape
    tm = PEER_TM
    n_i = t // tm
    last = n_i * nk - 1
    cur_i = lambda s: jnp.minimum(s, last) // nk
    cur_k = lambda s: jnp.minimum(s, last) % nk
    prev_i = lambda s: jnp.maximum(s - 1, 0) // nk
    prev_k = lambda s: jnp.maximum(s - 1, 0) % nk
    once = pl.Buffered(1)
    tab = lambda: pl.BlockSpec((PEER_HEADS, NKEYS, tm), lambda s: (0, 0, prev_i(s)), pipeline_mode=once)
    rows = lambda: pl.BlockSpec((PEER_HEADS, nk, NKEYS // nk, tm), lambda s: (0, 0, 0, prev_i(s)),
                                pipeline_mode=once)
    c1 = c1.reshape(PEER_HEADS, nk, NKEYS // nk, t)
    e1 = e1.reshape(PEER_HEADS, nk, NKEYS // nk, t)
    return pl.pallas_call(
        functools.partial(_peer_kernel, nk=nk, final=final),
        grid=(n_i * nk + 1,),
        in_specs=[pl.BlockSpec((d, tm), lambda s: (0, cur_i(s))),
                  pl.BlockSpec((ec, d), lambda s: (cur_k(s), 0)),
                  pl.BlockSpec((tm, d), lambda s: (prev_i(s), 0), pipeline_mode=once),
                  pl.BlockSpec((1, 1, N_MOD * d), lambda s: ((prev_i(s) * tm) // seq, 0, 0)),
                  rows(), rows(), tab(), tab(),
                  pl.BlockSpec((None, d, ec), lambda s: (prev_k(s), 0, 0)),
                  pl.BlockSpec((1, d), lambda s: (0, 0))],
        out_specs=pl.BlockSpec((tm, d), lambda s: (prev_i(s), 0), pipeline_mode=once),
        out_shape=jax.ShapeDtypeStruct((t, d), F32),
        scratch_shapes=[pltpu.VMEM((d, tm), F32),
                        pltpu.VMEM((ec, tm), F32),
                        pltpu.VMEM((ec, tm), F32)],
        compiler_params=_cparams("arbitrary"),
        name="peer_dense",
    )(h2, u, x, mod, c1, e1, r2, e2, vt, final_g)


def kernel(x, c, rel_bias, w_mod, b_mod, norm1_g, norm2_g, w_in, w_out, diff_lambda, subln_g, conf_dw, conf_ln_g, conf_ln_b, sconv_w, sgu_ln_g, sgu_ln_b, sgu_w, sgu_b, peer_wq, peer_keys, peer_u, peer_v, final_g):
    batch, seq, d = x.shape
    depth = w_in.shape[0]
    t = batch * seq
    assert w_in.shape[2] == N_SLICES * GROUP and seq % PEER_TM == 0 and d % LANES == 0

    xf = x.reshape(t, d)
    mods = _mod_call(c, w_mod, b_mod).reshape(depth, batch, 1, N_MOD * d)
    attn_bias = _attn_bias_tiles(rel_bias)
    row = lambda v: v.reshape(1, -1)

    for l in range(depth):
        lam_init = 0.8 - 0.6 * math.exp(-0.3 * l)
        wl = w_in[l]
        wqv_t = jnp.transpose(jnp.concatenate([wl[:, 0:GROUP], wl[:, 2 * GROUP:3 * GROUP]], axis=1)).astype(BF16)
        w_rest = jnp.concatenate([wl[:, GROUP:2 * GROUP], wl[:, 3 * GROUP:]], axis=1).astype(BF16)
        qt, k, vt, pc = _in_call(xf, mods[l], row(norm1_g[l]), wqv_t, w_rest, seq)
        yat = _attn_call(qt, k, vt, attn_bias, diff_lambda[l], subln_g[l].reshape(ATT_V, 1),
                         batch, seq, lam_init)
        sgu_bias_tile = jnp.repeat(jnp.transpose(sgu_b[l]), GROUP // SGU_HEADS, axis=1)
        ybcd = _conv_call(pc, conf_dw[l], row(conf_ln_g[l]), row(conf_ln_b[l]), sconv_w[l],
                          row(sgu_ln_g[l]), row(sgu_ln_b[l]), sgu_w[l], sgu_bias_tile, batch, seq)
        keys = peer_keys[l].reshape(2 * PEER_HEADS, NKEYS, -1).astype(BF16)
        xf, h2, c1, e1, r2, e2 = _outtop_call(xf, yat, ybcd, mods[l], w_out[l].astype(BF16),
                                              row(norm2_g[l]), peer_wq[l].astype(BF16), keys, seq)
        vt = jnp.transpose(peer_v[l].reshape(-1, PEER_EC, d), (0, 2, 1)).astype(BF16)
        xf = _peer_call(xf, h2, mods[l], c1, e1, r2, e2, peer_u[l].astype(BF16), vt, row(final_g),
                        seq, l == depth - 1)
    return xf.reshape(batch, seq, d)
```

```python
import functools
import math

import jax
import jax.numpy as jnp
import numpy as np
from jax import lax
from jax.experimental import pallas as pl
from jax.experimental.pallas import tpu as pltpu

F32 = jnp.float32
BF16 = jnp.bfloat16
I32 = jnp.int32
HIGHEST = lax.Precision.HIGHEST

GROUP = 256
N_SLICES = 10
ATT_HEADS = 4
ATT_V = 64
ATT_QK = 32
N_MAPS = 2 * ATT_HEADS
NUM_BUCKETS = 32
MAX_DISTANCE = 128
CONF_W = 31
SCONV_W = 3
SGU_HEADS = 4
SGU_CHUNK = 128
PEER_HEADS = 8
NKEYS = 128
TOPK = 16
N_MOD = 6
EPS = 1e-6
NEG_INF = -1e30
SQRT_HALF = float(np.sqrt(0.5))

LANES = 128
SUBLANES = 8
VMEM_LIMIT_BYTES = 56 * 1024 * 1024

PEER_TM = 1024
PEER_EC = 1024
PEER_TN = 256
ATQ = 256
ONES_ROWS = 16

CAND_COUNTS = [TOPK // (a + 1) for a in range(TOPK)]
CAND_STARTS = [int(v) for v in np.cumsum([0] + CAND_COUNTS[:-1])]
N_CAND = sum(CAND_COUNTS)
N_CAND_PAD = -(-N_CAND // SUBLANES) * SUBLANES

_NT = (((1,), (1,)), ((), ()))
_TN = (((0,), (0,)), ((), ()))


def _as_words(x):
    return pltpu.bitcast(x, jnp.uint32)


def _as_bf16(words):
    return pltpu.bitcast(words, BF16)


def _pack_row_pairs(x):
    *lead, rows, cols = x.shape
    pairs = jnp.swapaxes(x.reshape(*lead, rows // 2, 2, cols), -1, -2)
    return lax.bitcast_convert_type(pairs, jnp.uint32)


def _gelu(x):
    return 0.5 * x * (1.0 + lax.erf(x * SQRT_HALF))


def _cparams(*sem):
    return pltpu.CompilerParams(dimension_semantics=sem, vmem_limit_bytes=VMEM_LIMIT_BYTES)


def _mod_kernel(c_ref, w_ref, b_ref, o_ref):
    c = c_ref[...]
    cond = c * jax.nn.sigmoid(c)
    o_ref[0] = jnp.dot(cond, w_ref[0], precision=HIGHEST, preferred_element_type=F32) + b_ref[0]


def _mod_call(c, w_mod, b_mod):
    depth, d, n = w_mod.shape
    b = c.shape[0]
    tn = 2048
    return pl.pallas_call(
        _mod_kernel,
        grid=(depth, n // tn),
        in_specs=[pl.BlockSpec((b, d), lambda l, j: (0, 0)),
                  pl.BlockSpec((1, d, tn), lambda l, j: (l, 0, j)),
                  pl.BlockSpec((1, 1, tn), lambda l, j: (l, 0, j))],
        out_specs=pl.BlockSpec((1, b, tn), lambda l, j: (l, 0, j)),
        out_shape=jax.ShapeDtypeStruct((depth, b, n), F32),
        compiler_params=_cparams("arbitrary", "arbitrary"),
        name="mod",
    )(c, w_mod, b_mod.reshape(depth, 1, n))


def _modulated_norm(x, g, shift, scale):
    ms = jnp.mean(x * x, axis=-1, keepdims=True)
    return (x * lax.rsqrt(ms + EPS) * g) * (1.0 + scale) + shift


def _in_kernel(x_ref, mod_ref, g_ref, wqv_ref, wr_ref, qt_ref, k_ref, vt_ref, pc_ref):
    d = x_ref.shape[1]
    h = _modulated_norm(x_ref[...], g_ref[...], mod_ref[0, :, 0:d], mod_ref[0, :, d:2 * d]).astype(BF16)
    qv = lax.dot_general(wqv_ref[...], h, _NT, preferred_element_type=F32)
    qt_ref[...] = (qv[0:GROUP] * (ATT_QK ** -0.5)).astype(BF16)
    for c in range(vt_ref.shape[0]):
        vt_ref[c] = qv[GROUP:2 * GROUP, c * ATQ:(c + 1) * ATQ].astype(BF16)
    res = jnp.dot(h, wr_ref[...], preferred_element_type=F32)
    k_ref[...] = res[:, 0:GROUP].astype(BF16)
    pc_ref[...] = res[:, GROUP:]


def _in_call(x, mod, g, wqv_t, w_rest, seq):
    t, d = x.shape
    n = w_rest.shape[1]
    tm = 512
    return pl.pallas_call(
        _in_kernel,
        grid=(t // tm,),
        in_specs=[pl.BlockSpec((tm, d), lambda i: (i, 0)),
                  pl.BlockSpec((1, 1, N_MOD * d), lambda i: ((i * tm) // seq, 0, 0)),
                  pl.BlockSpec((1, d), lambda i: (0, 0)),
                  pl.BlockSpec((2 * GROUP, d), lambda i: (0, 0)),
                  pl.BlockSpec((d, n), lambda i: (0, 0))],
        out_specs=[pl.BlockSpec((GROUP, tm), lambda i: (0, i)),
                   pl.BlockSpec((tm, GROUP), lambda i: (i, 0)),
                   pl.BlockSpec((tm // ATQ, GROUP, ATQ), lambda i: (i, 0, 0)),
                   pl.BlockSpec((tm, n - GROUP), lambda i: (i, 0))],
        out_shape=[jax.ShapeDtypeStruct((GROUP, t), BF16),
                   jax.ShapeDtypeStruct((t, GROUP), BF16),
                   jax.ShapeDtypeStruct((t // ATQ, GROUP, ATQ), BF16),
                   jax.ShapeDtypeStruct((t, n - GROUP), F32)],
        compiler_params=_cparams("arbitrary"),
        name="in_proj",
    )(x, mod, g, wqv_t, w_rest)


def _attn_kernel(qt_ref, k_ref, vt_ref, bias_ref, lam_ref, g_ref, o_ref, qz_ref, acc_ref, m_ref, *, lam_init):
    i = pl.program_id(1)
    qt = qt_ref[...]
    rowi = lax.broadcasted_iota(I32, (GROUP, ATQ), 0)
    for c in range(N_MAPS):
        lo = c * ATT_QK
        qz_ref[c] = jnp.where((rowi >= lo) & (rowi < lo + ATT_QK), qt, jnp.zeros_like(qt))
    ones = jnp.ones((ONES_ROWS, ATQ), BF16)

    def step(jb, bias_idx, init):
        kb = k_ref[pl.ds(pl.multiple_of(jb * ATQ, ATQ), ATQ), :]
        vb = vt_ref[jb]
        m_old = None if init else m_ref[...]
        acc_old = None if init else [acc_ref[c] for c in range(N_MAPS)]
        scores = [jnp.dot(kb, qz_ref[c], preferred_element_type=F32) for c in range(N_MAPS)]
        m_new, probs, alphas = [], [], []
        for c in range(N_MAPS):
            s = scores[c]
            if bias_idx is not None:
                s = s + bias_ref[c // 2, bias_idx]
            mn = jnp.max(s, axis=0, keepdims=True)
            if not init:
                mo = m_old[c:c + 1, :]
                mn = jnp.maximum(mo, mn)
                alphas.append(jnp.exp(mo - mn))
            probs.append(jnp.exp(s - mn).astype(BF16))
            m_new.append(mn)
        acc_new = []
        for c in range(N_MAPS):
            h = c // 2
            vaug = jnp.concatenate([vb[h * ATT_V:(h + 1) * ATT_V, :], ones], axis=0)
            pv = jnp.dot(vaug, probs[c], preferred_element_type=F32)
            acc_new.append(pv if init else alphas[c] * acc_old[c] + pv)
        m_ref[...] = jnp.concatenate(m_new, axis=0)
        for c in range(N_MAPS):
            acc_ref[c] = acc_new[c]

    step(i, 1, True)

    @pl.when(i >= 1)
    def _():
        step(i - 1, 0, False)

    def far(j, carry):
        step(j, None, False)
        return carry

    lax.fori_loop(0, jnp.maximum(i - 1, 0), far, 0)

    lp = lam_ref[...]
    lam = (jnp.exp(jnp.sum(lp[0:1] * lp[1:2], axis=1, keepdims=True))
           - jnp.exp(jnp.sum(lp[2:3] * lp[3:4], axis=1, keepdims=True)) + lam_init)
    for h in range(ATT_HEADS):
        a0 = acc_ref[2 * h]
        a1 = acc_ref[2 * h + 1]
        o = a0[0:ATT_V] / a0[ATT_V:ATT_V + 1] - lam * (a1[0:ATT_V] / a1[ATT_V:ATT_V + 1])
        ms = jnp.mean(o * o, axis=0, keepdims=True)
        y = o * lax.rsqrt(ms + EPS) * g_ref[...] * (1.0 - lam_init)
        o_ref[h * ATT_V:(h + 1) * ATT_V, :] = y.astype(o_ref.dtype)


def _attn_call(qt, k, vt, bias, lam_par, subln_col, batch, seq, lam_init):
    t = k.shape[0]
    nq = seq // ATQ
    return pl.pallas_call(
        functools.partial(_attn_kernel, lam_init=lam_init),
        grid=(batch, nq),
        in_specs=[pl.BlockSpec((GROUP, ATQ), lambda b, i: (0, b * nq + i)),
                  pl.BlockSpec((seq, GROUP), lambda b, i: (b, 0)),
                  pl.BlockSpec((nq, GROUP, ATQ), lambda b, i: (b, 0, 0)),
                  pl.BlockSpec((ATT_HEADS, 2, ATQ, ATQ), lambda b, i: (0, 0, 0, 0)),
                  pl.BlockSpec((4, ATT_QK), lambda b, i: (0, 0)),
                  pl.BlockSpec((ATT_V, 1), lambda b, i: (0, 0))],
        out_specs=pl.BlockSpec((GROUP, ATQ), lambda b, i: (0, b * nq + i)),
        out_shape=jax.ShapeDtypeStruct((GROUP, t), BF16),
        scratch_shapes=[pltpu.VMEM((N_MAPS, GROUP, ATQ), BF16),
                        pltpu.VMEM((N_MAPS, ATT_V + ONES_ROWS, ATQ), F32),
                        pltpu.VMEM((N_MAPS, ATQ), F32)],
        compiler_params=_cparams("arbitrary", "arbitrary"),
        name="diff_attn",
    )(qt, k, vt, bias, lam_par, subln_col)


def _rel_bucket_np(n):
    max_exact = NUM_BUCKETS // 2
    n = np.maximum(n, 0)
    ratio = (np.log(np.maximum(n, 1).astype(np.float32) / np.float32(max_exact))
             / np.float32(math.log(MAX_DISTANCE / max_exact)))
    large = np.minimum(max_exact + (ratio * (NUM_BUCKETS - max_exact)).astype(np.int32), NUM_BUCKETS - 1)
    return np.where(n < max_exact, n, large).astype(np.int32)


def _bias_kernel(bucket_ref, rb_ref, o_ref):
    h = pl.program_id(0)
    far = rb_ref[NUM_BUCKETS - 1, h]
    for tile in range(2):
        b = bucket_ref[tile]
        acc = jnp.full(b.shape, NEG_INF, F32)
        for k in range(NUM_BUCKETS):
            acc = jnp.where(b == k, rb_ref[k, h] - far, acc)
        o_ref[0, tile] = acc


def _attn_bias_tiles(rel_bias):
    rel_cur = np.arange(ATQ)[None, :] - np.arange(ATQ)[:, None]
    assert np.all(_rel_bucket_np(np.arange(ATQ + 1, 8 * ATQ)) == NUM_BUCKETS - 1)
    buckets = np.stack([_rel_bucket_np(rel_cur + ATQ),
                        np.where(rel_cur >= 0, _rel_bucket_np(rel_cur), NUM_BUCKETS)]).astype(np.int32)
    return pl.pallas_call(
        _bias_kernel,
        grid=(ATT_HEADS,),
        in_specs=[pl.BlockSpec((2, ATQ, ATQ), lambda h: (0, 0, 0)),
                  pl.BlockSpec(memory_space=pltpu.SMEM)],
        out_specs=pl.BlockSpec((1, 2, ATQ, ATQ), lambda h: (h, 0, 0, 0)),
        out_shape=jax.ShapeDtypeStruct((ATT_HEADS, 2, ATQ, ATQ), F32),
        compiler_params=_cparams("arbitrary"),
        name="attn_bias",
    )(jnp.asarray(buckets), rel_bias)


CONV_HALO = 32
SCONV_HALO = 8


def _layer_norm(x, g, b):
    mu = jnp.mean(x, axis=-1, keepdims=True)
    xc = x - mu
    var = jnp.mean(xc * xc, axis=-1, keepdims=True)
    return xc * lax.rsqrt(var + EPS) * g + b


def _conv_kernel(cur_ref, hga_ref, hgb_ref, hcc_ref, hch_ref, dw_ref, lng_ref, lnb_ref, sw_ref,
                 sg_ref, sb_ref, ws_ref, bs_ref, o_ref, ext_ref, ext2_ref):
    i = pl.program_id(1)
    ts = cur_ref.shape[0]
    first = i == 0

    def col(k):
        return cur_ref[:, k * GROUP:(k + 1) * GROUP]

    zh = hga_ref[...] * jax.nn.sigmoid(hgb_ref[...])
    ext_ref[0:CONV_HALO, :] = jnp.where(first, 0.0, zh)
    ext_ref[CONV_HALO:, :] = col(0) * jax.nn.sigmoid(col(1))
    acc = jnp.zeros((ts, GROUP), F32)
    for j in range(CONF_W):
        off = CONV_HALO - (CONF_W - 1) + j
        acc = acc + dw_ref[j:j + 1, :] * ext_ref[off:off + ts, :]
    zb = _layer_norm(acc, lng_ref[...], lnb_ref[...])
    o_ref[:, 0:GROUP] = (zb * jax.nn.sigmoid(zb)).astype(o_ref.dtype)

    ext2_ref[0:SCONV_HALO, :] = jnp.where(first, 0.0, hcc_ref[...] * hch_ref[...])
    ext2_ref[SCONV_HALO:, :] = col(3) * col(4)
    acc = jnp.zeros((ts, GROUP), F32)
    for j in range(SCONV_W):
        off = SCONV_HALO - (SCONV_W - 1) + j
        acc = acc + sw_ref[j:j + 1, :] * ext2_ref[off:off + ts, :]
    o_ref[:, GROUP:2 * GROUP] = (col(2) * acc).astype(o_ref.dtype)

    lane = lax.broadcasted_iota(I32, (1, GROUP), 1)
    rr = lax.broadcasted_iota(I32, (SGU_CHUNK, SGU_CHUNK), 0)
    cc = lax.broadcasted_iota(I32, (SGU_CHUNK, SGU_CHUNK), 1)
    hd = GROUP // SGU_HEADS
    w_tril = [jnp.where(rr >= cc, ws_ref[h], 0.0).astype(BF16) for h in range(SGU_HEADS)]
    for c in range(ts // SGU_CHUNK):
        rows = slice(c * SGU_CHUNK, (c + 1) * SGU_CHUNK)
        u = _gelu(cur_ref[rows, 5 * GROUP:6 * GROUP])
        v = _layer_norm(_gelu(cur_ref[rows, 6 * GROUP:7 * GROUP]), sg_ref[...], sb_ref[...]).astype(BF16)
        z = bs_ref[...]
        for h in range(SGU_HEADS):
            zh_ = jnp.dot(w_tril[h], v, preferred_element_type=F32)
            z = z + jnp.where((lane >= h * hd) & (lane < (h + 1) * hd), zh_, 0.0)
        o_ref[rows, 2 * GROUP:3 * GROUP] = (u * z).astype(o_ref.dtype)


def _conv_call(pc, conf_dw, conf_ln_g, conf_ln_b, sconv_w, sgu_ln_g, sgu_ln_b, sgu_w, sgu_bias_tile,
               batch, seq):
    t, n = pc.shape
    ts = 512
    nst = seq // ts

    def halo_map(rows, colblk):
        per = ts // rows
        return lambda b, i: (jnp.maximum((b * nst + i) * per - 1, 0), colblk)

    vec = lambda: pl.BlockSpec((1, GROUP), lambda b, i: (0, 0))
    return pl.pallas_call(
        _conv_kernel,
        grid=(batch, nst),
        in_specs=[pl.BlockSpec((ts, n), lambda b, i: (b * nst + i, 0)),
                  pl.BlockSpec((CONV_HALO, GROUP), halo_map(CONV_HALO, 0)),
                  pl.BlockSpec((CONV_HALO, GROUP), halo_map(CONV_HALO, 1)),
                  pl.BlockSpec((SCONV_HALO, GROUP), halo_map(SCONV_HALO, 3)),
                  pl.BlockSpec((SCONV_HALO, GROUP), halo_map(SCONV_HALO, 4)),
                  pl.BlockSpec((CONF_W, GROUP), lambda b, i: (0, 0)),
                  vec(), vec(),
                  pl.BlockSpec((SCONV_W, GROUP), lambda b, i: (0, 0)),
                  vec(), vec(),
                  pl.BlockSpec((SGU_HEADS, SGU_CHUNK, SGU_CHUNK), lambda b, i: (0, 0, 0)),
                  pl.BlockSpec((SGU_CHUNK, GROUP), lambda b, i: (0, 0))],
        out_specs=pl.BlockSpec((ts, 3 * GROUP), lambda b, i: (b * nst + i, 0)),
        out_shape=jax.ShapeDtypeStruct((t, 3 * GROUP), BF16),
        scratch_shapes=[pltpu.VMEM((CONV_HALO + ts, GROUP), F32),
                        pltpu.VMEM((SCONV_HALO + ts, GROUP), F32)],
        compiler_params=_cparams("arbitrary", "arbitrary"),
        name="conv_mix",
    )(pc, pc, pc, pc, pc, conf_dw, conf_ln_g, conf_ln_b, sconv_w, sgu_ln_g, sgu_ln_b, sgu_w,
      sgu_bias_tile)


def _extract_top(s, rowi, n_rounds, want_rank):
    sentinel = float(s.shape[0])
    vals = []
    mark = jnp.full(s.shape, float(n_rounds) if want_rank else 0.0, F32)
    for r in range(n_rounds):
        m = jnp.max(s, axis=0, keepdims=True)
        idx = jnp.min(jnp.where(s == m, rowi, sentinel), axis=0, keepdims=True)
        hit = rowi == idx
        mark = jnp.where(hit, float(r) if want_rank else 1.0, mark)
        s = jnp.where(hit, -jnp.inf, s)
        vals.append(m)
    return vals, mark


def _batcher_network(n):
    pairs = []

    def merge(lo, m, r):
        step = 2 * r
        if step < m:
            merge(lo, m, step)
            merge(lo + r, m, step)
            pairs.extend((i, i + r) for i in range(lo + r, lo + m - r, step))
        else:
            pairs.append((lo, lo + r))

    def sort(lo, m):
        if m > 1:
            sort(lo, m // 2)
            sort(lo + m // 2, m // 2)
            merge(lo, m, 1)

    sort(0, n)
    return pairs


def _compare_exchange(blocks, i, j):
    hi, lo = jnp.maximum(blocks[i], blocks[j]), jnp.minimum(blocks[i], blocks[j])
    blocks[i], blocks[j] = hi, lo


def _bitonic_merge(blocks):
    n = len(blocks)
    d = n // 2
    while d >= 1:
        for i in range(n):
            if i & d == 0:
                _compare_exchange(blocks, i, i + d)
        d //= 2
    return blocks


def _sorted_top(s):
    n = s.shape[0] // SUBLANES
    assert n == TOPK
    blocks = [s[j * SUBLANES:(j + 1) * SUBLANES, :] for j in range(n)]
    for i, j in _batcher_network(n):
        _compare_exchange(blocks, i, j)
    shift = SUBLANES // 2
    while shift >= 1:
        rolled = [pltpu.roll(b, shift, 0) for b in blocks]
        blocks = _bitonic_merge([jnp.maximum(blocks[j], rolled[n - 1 - j]) for j in range(n)])
        shift //= 2
    return blocks


def _extract_values(s, n_rounds):
    vals = []
    for _ in range(n_rounds):
        m = jnp.max(s, axis=0, keepdims=True)
        s = jnp.where(s == m, -jnp.inf, s)
        vals.append(m)
    return vals


def _tile_rows(block, n_rows):
    return jnp.concatenate([block] * (n_rows // block.shape[0]), axis=0)


def _pack_pair(x):
    bits = pltpu.bitcast(x.astype(BF16).astype(F32), jnp.uint32)
    return bits | (bits >> 16)


def _outtop_kernel(x_ref, yat_ref, yb_ref, mod_ref, wo_ref, g_ref, wq_ref, keys_ref,
                   xo_ref, h2t_ref, c1_ref, e1_ref, r2_ref, e2_ref, qs_ref):
    d = x_ref.shape[1]
    tm = x_ref.shape[0]
    y = lax.dot_general(yat_ref[...], wo_ref[0:GROUP, :], _TN, preferred_element_type=F32)
    y = y + jnp.dot(yb_ref[...], wo_ref[GROUP:, :], preferred_element_type=F32)
    x = x_ref[...] + mod_ref[0, :, 2 * d:3 * d] * y
    xo_ref[...] = x
    h2f = _modulated_norm(x, g_ref[...], mod_ref[0, :, 3 * d:4 * d], mod_ref[0, :, 4 * d:5 * d])
    h2 = h2f.astype(BF16)
    h2t_ref[...] = _as_words(jnp.transpose(h2f).astype(BF16))
    q = jnp.dot(h2, wq_ref[...], preferred_element_type=F32).astype(BF16)
    for hp in range(2 * PEER_HEADS):
        qs_ref[hp] = q[:, hp * NKEYS:(hp + 1) * NKEYS]

    rowi = lax.broadcasted_iota(I32, (NKEYS, tm), 0).astype(F32)
    rowc = lax.broadcasted_iota(I32, (N_CAND_PAD, tm), 0)
    rowc_f = rowc.astype(F32)
    a_row = jnp.zeros((N_CAND_PAD, tm), I32) - 1
    for st in CAND_STARTS:
        a_row = a_row + (rowc >= st).astype(I32)
    a_row = jnp.where(rowc >= N_CAND, TOPK, a_row)
    start_row = jnp.zeros((N_CAND_PAD, tm), I32)
    for a, st in enumerate(CAND_STARTS):
        start_row = jnp.where(a_row == a, st, start_row)
    b_row = rowc - start_row

    def candidates(v1, v2):
        rep1 = jnp.full((N_CAND_PAD, tm), -jnp.inf, F32)
        rep2 = jnp.full((N_CAND_PAD, tm), -jnp.inf, F32)
        for a in range(TOPK):
            rep1 = jnp.where(a_row == a, v1[a], rep1)
            rep2 = jnp.where(b_row == a, v2[a], rep2)
        return rep1 + rep2

    def group_counts(sel):
        return [jnp.sum(sel[st:st + n], axis=0, keepdims=True) for st, n in zip(CAND_STARTS, CAND_COUNTS)]

    def store(hh, s1, s2, max1, max2, z, c1, r2):
        c1_ref[hh] = _pack_pair(c1)
        e1_ref[hh] = _pack_pair(jnp.exp(s1 - max1) / z)
        r2_ref[hh] = _as_words(r2.astype(BF16))
        e2_ref[hh] = _as_words(jnp.exp(s2 - max2).astype(BF16))

    def head(hh, carry):
        s1, s2 = [lax.dot_general(keys_ref[2 * hh + p], qs_ref[2 * hh + p], _NT,
                                  preferred_element_type=F32) for p in range(2)]

        v1 = _sorted_top(s1)
        v2 = _sorted_top(s2)
        cand = candidates([v[0:1] for v in v1], [v[0:1] for v in v2])
        top = _extract_values(cand, TOPK)
        cmax, thr = top[0], top[TOPK - 1]
        z = jnp.zeros_like(cmax)
        for val in top:
            z = z + jnp.exp(val - cmax)
        counts = group_counts((cand >= thr).astype(F32))
        c1 = jnp.zeros((NKEYS, tm), F32)
        r2 = jnp.full((NKEYS, tm), float(TOPK), F32)
        for a in reversed(range(TOPK)):
            c1 = jnp.where(s1 == _tile_rows(v1[a], NKEYS), counts[a], c1)
            r2 = jnp.where(s2 == _tile_rows(v2[a], NKEYS), float(a), r2)
        store(hh, s1, s2, v1[0][0:1], v2[0][0:1], z, c1, r2)

        tied = jnp.sum(c1, axis=0, keepdims=True) != float(TOPK)
        for s, v in ((s1, v1), (s2, v2)):
            dup = v[0] == v[1]
            for a in range(1, TOPK - 1):
                dup = dup | (v[a] == v[a + 1])
            n_top = jnp.sum((s >= _tile_rows(v[TOPK - 1], NKEYS)).astype(F32), axis=0, keepdims=True)
            tied = tied | (n_top != float(TOPK)) | (jnp.max(dup.astype(F32), axis=0, keepdims=True) > 0.0)

        @pl.when(jnp.max(tied.astype(F32)) > 0.0)
        def _():
            w1, rank1 = _extract_top(s1, rowi, TOPK, True)
            w2, rank2 = _extract_top(s2, rowi, TOPK, True)
            cand_x = candidates(w1, w2)
            _, sel = _extract_top(cand_x, rowc_f, TOPK, False)
            cmax_x = w1[0] + w2[0]
            z_x = jnp.sum(sel * jnp.exp(jnp.where(sel > 0.0, cand_x, cmax_x) - cmax_x), axis=0, keepdims=True)
            counts_x = group_counts(sel)
            c1_x = jnp.zeros((NKEYS, tm), F32)
            for a in range(TOPK):
                c1_x = jnp.where(rank1 == float(a), counts_x[a], c1_x)
            store(hh, s1, s2, w1[0], w2[0], z_x, c1_x, rank2)

        return carry

    lax.fori_loop(0, PEER_HEADS, head, 0)


def _outtop_call(x, yat, ybcd, mod, w_out, g2, wq, keys, seq):
    t, d = x.shape
    tm = 256
    nq = wq.shape[1]
    tab = lambda: pl.BlockSpec((PEER_HEADS, NKEYS, tm), lambda i: (0, 0, i))
    half = lambda: pl.BlockSpec((PEER_HEADS, NKEYS // 2, tm), lambda i: (0, 0, i))
    pair_shape = jax.ShapeDtypeStruct((PEER_HEADS, NKEYS, t), jnp.uint32)
    half_shape = jax.ShapeDtypeStruct((PEER_HEADS, NKEYS // 2, t), jnp.uint32)
    return pl.pallas_call(
        _outtop_kernel,
        grid=(t // tm,),
        in_specs=[pl.BlockSpec((tm, d), lambda i: (i, 0)),
                  pl.BlockSpec((GROUP, tm), lambda i: (0, i)),
                  pl.BlockSpec((tm, 3 * GROUP), lambda i: (i, 0)),
                  pl.BlockSpec((1, 1, N_MOD * d), lambda i: ((i * tm) // seq, 0, 0)),
                  pl.BlockSpec((4 * GROUP, d), lambda i: (0, 0)),
                  pl.BlockSpec((1, d), lambda i: (0, 0)),
                  pl.BlockSpec((d, nq), lambda i: (0, 0)),
                  pl.BlockSpec((2 * PEER_HEADS, NKEYS, NKEYS), lambda i: (0, 0, 0))],
        out_specs=[pl.BlockSpec((tm, d), lambda i: (i, 0)),
                   pl.BlockSpec((d // 2, tm), lambda i: (0, i)),
                   tab(), tab(), half(), half()],
        out_shape=[jax.ShapeDtypeStruct((t, d), F32),
                   jax.ShapeDtypeStruct((d // 2, t), jnp.uint32),
                   pair_shape, pair_shape, half_shape, half_shape],
        scratch_shapes=[pltpu.VMEM((2 * PEER_HEADS, tm, NKEYS), BF16)],
        compiler_params=_cparams("arbitrary"),
        name="out_topk",
    )(x, yat, ybcd, mod, w_out, g2, wq, keys)


def _bcast_pair_rows(row_u32, n_rows):
    tile = pltpu.bitcast(jnp.broadcast_to(row_u32, (SUBLANES, row_u32.shape[1])), BF16)
    return jnp.concatenate([tile] * (n_rows // (2 * SUBLANES)), axis=0)


def _peer_kernel(h2t_ref, u_ref, x_ref, mod_ref, c1_ref, e1_ref, r2_ref, e2_ref, vt_ref, fg_ref,
                 o_ref, acc_ref, a0_ref, a1_ref, *, nk, final):
    s = pl.program_id(0)
    kp = jnp.maximum(s - 1, 0) % nk
    nb = vt_ref.shape[1] // NKEYS
    d = x_ref.shape[1]

    @pl.when(s == 0)
    def _():
        a1_ref[...] = jnp.zeros_like(a1_ref)

    @pl.when(kp == 0)
    def _():
        acc_ref[...] = jnp.zeros_like(acc_ref)

    n_col = x_ref.shape[0] // PEER_TN

    def body(a_cur, a_prev):
        def cols(j):
            return slice(j * PEER_TN, (j + 1) * PEER_TN)

        def first_matmul(j):
            a_cur[:, cols(j)] = jnp.dot(_as_bf16(u_ref[...]), _as_bf16(h2t_ref[:, cols(j)]),
                                        preferred_element_type=F32)

        def gated_act(j):
            blocks = []
            for ii in range(nb):
                act = _gelu(a_prev[ii * NKEYS:(ii + 1) * NKEYS, cols(j)]).astype(BF16)
                gate = None
                for hh in range(PEER_HEADS):
                    c1 = _bcast_pair_rows(c1_ref[hh, kp, ii:ii + 1, cols(j)], NKEYS)
                    e1 = _bcast_pair_rows(e1_ref[hh, kp, ii:ii + 1, cols(j)], NKEYS)
                    g = jnp.where(_as_bf16(r2_ref[hh, :, cols(j)]) < c1, _as_bf16(e2_ref[hh, :, cols(j)]) * e1,
                                  jnp.zeros_like(e1))
                    gate = g if gate is None else gate + g
                blocks.append(gate * act)
            return jnp.concatenate(blocks, axis=0)

        def second_matmul(j, w):
            acc_ref[:, cols(j)] += jnp.dot(_as_bf16(vt_ref[...]), w, preferred_element_type=F32)

        first_matmul(0)
        for j in range(n_col):
            if j + 1 < n_col:
                first_matmul(j + 1)
            second_matmul(j, gated_act(j))

    @pl.when(s % 2 == 0)
    def _():
        body(a0_ref, a1_ref)

    @pl.when(s % 2 == 1)
    def _():
        body(a1_ref, a0_ref)

    @pl.when((kp == nk - 1) & (s >= 1))
    def _():
        y = jnp.transpose(acc_ref[...])
        x = x_ref[...] + mod_ref[0, :, 5 * d:6 * d] * y
        if final:
            ms = jnp.mean(x * x, axis=-1, keepdims=True)
            x = x * lax.rsqrt(ms + EPS) * fg_ref[...]
        o_ref[...] = x


def _peer_call(x, h2, mod, c1, e1, r2, e2, u, vt, final_g, seq, final):
    t, d = x.shape
    nk, _, ec = vt.shape
    tm = PEER_TM
    n_i = t // tm
    last = n_i * nk - 1
    cur_i = lambda s: jnp.minimum(s, last) // nk
    cur_k = lambda s: jnp.minimum(s, last) % nk
    prev_i = lambda s: jnp.maximum(s - 1, 0) // nk
    prev_k = lambda s: jnp.maximum(s - 1, 0) % nk
    once = pl.Buffered(1)
    tab = lambda: pl.BlockSpec((PEER_HEADS, NKEYS // 2, tm), lambda s: (0, 0, prev_i(s)), pipeline_mode=once)
    rows = lambda: pl.BlockSpec((PEER_HEADS, nk, NKEYS // nk, tm), lambda s: (0, 0, 0, prev_i(s)),
                                pipeline_mode=once)
    c1 = c1.reshape(PEER_HEADS, nk, NKEYS // nk, t)
    e1 = e1.reshape(PEER_HEADS, nk, NKEYS // nk, t)
    return pl.pallas_call(
        functools.partial(_peer_kernel, nk=nk, final=final),
        grid=(n_i * nk + 1,),
        in_specs=[pl.BlockSpec((d // 2, tm), lambda s: (0, cur_i(s))),
                  pl.BlockSpec((ec // 2, d), lambda s: (cur_k(s), 0)),
                  pl.BlockSpec((tm, d), lambda s: (prev_i(s), 0), pipeline_mode=once),
                  pl.BlockSpec((1, 1, N_MOD * d), lambda s: ((prev_i(s) * tm) // seq, 0, 0)),
                  rows(), rows(), tab(), tab(),
                  pl.BlockSpec((None, d // 2, ec), lambda s: (prev_k(s), 0, 0)),
                  pl.BlockSpec((1, d), lambda s: (0, 0))],
        out_specs=pl.BlockSpec((tm, d), lambda s: (prev_i(s), 0), pipeline_mode=once),
        out_shape=jax.ShapeDtypeStruct((t, d), F32),
        scratch_shapes=[pltpu.VMEM((d, tm), F32),
                        pltpu.VMEM((ec, tm), F32),
                        pltpu.VMEM((ec, tm), F32)],
        compiler_params=_cparams("arbitrary"),
        name="peer_dense",
    )(h2, u, x, mod, c1, e1, r2, e2, vt, final_g)


def kernel(x, c, rel_bias, w_mod, b_mod, norm1_g, norm2_g, w_in, w_out, diff_lambda, subln_g, conf_dw, conf_ln_g, conf_ln_b, sconv_w, sgu_ln_g, sgu_ln_b, sgu_w, sgu_b, peer_wq, peer_keys, peer_u, peer_v, final_g):
    batch, seq, d = x.shape
    depth = w_in.shape[0]
    t = batch * seq
    assert w_in.shape[2] == N_SLICES * GROUP and seq % PEER_TM == 0 and d % LANES == 0

    xf = x.reshape(t, d)
    mods = _mod_call(c, w_mod, b_mod).reshape(depth, batch, 1, N_MOD * d)
    attn_bias = _attn_bias_tiles(rel_bias)
    row = lambda v: v.reshape(1, -1)

    for l in range(depth):
        lam_init = 0.8 - 0.6 * math.exp(-0.3 * l)
        wl = w_in[l]
        wqv_t = jnp.transpose(jnp.concatenate([wl[:, 0:GROUP], wl[:, 2 * GROUP:3 * GROUP]], axis=1)).astype(BF16)
        w_rest = jnp.concatenate([wl[:, GROUP:2 * GROUP], wl[:, 3 * GROUP:]], axis=1).astype(BF16)
        qt, k, vt, pc = _in_call(xf, mods[l], row(norm1_g[l]), wqv_t, w_rest, seq)
        yat = _attn_call(qt, k, vt, attn_bias, diff_lambda[l], subln_g[l].reshape(ATT_V, 1),
                         batch, seq, lam_init)
        sgu_bias_tile = jnp.repeat(jnp.transpose(sgu_b[l]), GROUP // SGU_HEADS, axis=1)
        ybcd = _conv_call(pc, conf_dw[l], row(conf_ln_g[l]), row(conf_ln_b[l]), sconv_w[l],
                          row(sgu_ln_g[l]), row(sgu_ln_b[l]), sgu_w[l], sgu_bias_tile, batch, seq)
        keys = peer_keys[l].reshape(2 * PEER_HEADS, NKEYS, -1).astype(BF16)
        xf, h2, c1, e1, r2, e2 = _outtop_call(xf, yat, ybcd, mods[l], w_out[l].astype(BF16),
                                              row(norm2_g[l]), peer_wq[l].astype(BF16), keys, seq)
        vt = _pack_row_pairs(jnp.transpose(peer_v[l].reshape(-1, PEER_EC, d), (0, 2, 1)).astype(BF16))
        xf = _peer_call(xf, h2, mods[l], c1, e1, r2, e2, _pack_row_pairs(peer_u[l].astype(BF16)), vt, row(final_g),
                        seq, l == depth - 1)
    return xf.reshape(batch, seq, d)
```

```python
import functools
import math

import jax
import jax.numpy as jnp
import numpy as np
from jax import lax
from jax.experimental import pallas as pl
from jax.experimental.pallas import tpu as pltpu

F32 = jnp.float32
BF16 = jnp.bfloat16
I32 = jnp.int32
HIGHEST = lax.Precision.HIGHEST

GROUP = 256
N_SLICES = 10
ATT_HEADS = 4
ATT_V = 64
ATT_QK = 32
N_MAPS = 2 * ATT_HEADS
NUM_BUCKETS = 32
MAX_DISTANCE = 128
CONF_W = 31
SCONV_W = 3
SGU_HEADS = 4
SGU_CHUNK = 128
PEER_HEADS = 8
NKEYS = 128
TOPK = 16
N_MOD = 6
EPS = 1e-6
NEG_INF = -1e30
SQRT_HALF = float(np.sqrt(0.5))

LANES = 128
SUBLANES = 8
VMEM_LIMIT_BYTES = 56 * 1024 * 1024

PEER_TM = 1024
PEER_EC = 1024
PEER_TN = 256
ATQ = 256
ONES_ROWS = 16

CAND_COUNTS = [TOPK // (a + 1) for a in range(TOPK)]
CAND_STARTS = [int(v) for v in np.cumsum([0] + CAND_COUNTS[:-1])]
N_CAND = sum(CAND_COUNTS)
N_CAND_PAD = -(-N_CAND // SUBLANES) * SUBLANES

_NT = (((1,), (1,)), ((), ()))
_TN = (((0,), (0,)), ((), ()))


def _as_words(x):
    return pltpu.bitcast(x, jnp.uint32)


def _as_bf16(words):
    return pltpu.bitcast(words, BF16)


def _gelu(x):
    return 0.5 * x * (1.0 + lax.erf(x * SQRT_HALF))


def _cparams(*sem):
    return pltpu.CompilerParams(dimension_semantics=sem, vmem_limit_bytes=VMEM_LIMIT_BYTES)


def _mod_kernel(c_ref, w_ref, b_ref, o_ref):
    c = c_ref[...]
    cond = c * jax.nn.sigmoid(c)
    o_ref[0] = jnp.dot(cond.astype(BF16), w_ref[0].astype(BF16), preferred_element_type=F32) + b_ref[0]


def _mod_call(c, w_mod, b_mod):
    depth, d, n = w_mod.shape
    b = c.shape[0]
    tn = 2048
    return pl.pallas_call(
        _mod_kernel,
        grid=(depth, n // tn),
        in_specs=[pl.BlockSpec((b, d), lambda l, j: (0, 0)),
                  pl.BlockSpec((1, d, tn), lambda l, j: (l, 0, j)),
                  pl.BlockSpec((1, 1, tn), lambda l, j: (l, 0, j))],
        out_specs=pl.BlockSpec((1, b, tn), lambda l, j: (l, 0, j)),
        out_shape=jax.ShapeDtypeStruct((depth, b, n), F32),
        compiler_params=_cparams("arbitrary", "arbitrary"),
        name="mod",
    )(c, w_mod, b_mod.reshape(depth, 1, n))


def _modulated_norm(x, g, shift, scale):
    ms = jnp.mean(x * x, axis=-1, keepdims=True)
    return (x * lax.rsqrt(ms + EPS) * g) * (1.0 + scale) + shift


def _in_kernel(x_ref, mod_ref, g_ref, wqv_ref, wr_ref, qt_ref, k_ref, vt_ref, pc_ref):
    d = x_ref.shape[1]
    h = _modulated_norm(x_ref[...], g_ref[...], mod_ref[0, :, 0:d], mod_ref[0, :, d:2 * d]).astype(BF16)
    qv = lax.dot_general(wqv_ref[...], h, _NT, preferred_element_type=F32)
    qt_ref[...] = (qv[0:GROUP] * (ATT_QK ** -0.5)).astype(BF16)
    for c in range(vt_ref.shape[0]):
        vt_ref[c] = qv[GROUP:2 * GROUP, c * ATQ:(c + 1) * ATQ].astype(BF16)
    res = jnp.dot(h, wr_ref[...], preferred_element_type=F32)
    k_ref[...] = res[:, 0:GROUP].astype(BF16)
    pc_ref[...] = res[:, GROUP:]


def _in_call(x, mod, g, wqv_t, w_rest, seq):
    t, d = x.shape
    n = w_rest.shape[1]
    tm = 512
    return pl.pallas_call(
        _in_kernel,
        grid=(t // tm,),
        in_specs=[pl.BlockSpec((tm, d), lambda i: (i, 0)),
                  pl.BlockSpec((1, 1, N_MOD * d), lambda i: ((i * tm) // seq, 0, 0)),
                  pl.BlockSpec((1, d), lambda i: (0, 0)),
                  pl.BlockSpec((2 * GROUP, d), lambda i: (0, 0)),
                  pl.BlockSpec((d, n), lambda i: (0, 0))],
        out_specs=[pl.BlockSpec((GROUP, tm), lambda i: (0, i)),
                   pl.BlockSpec((tm, GROUP), lambda i: (i, 0)),
                   pl.BlockSpec((tm // ATQ, GROUP, ATQ), lambda i: (i, 0, 0)),
                   pl.BlockSpec((tm, n - GROUP), lambda i: (i, 0))],
        out_shape=[jax.ShapeDtypeStruct((GROUP, t), BF16),
                   jax.ShapeDtypeStruct((t, GROUP), BF16),
                   jax.ShapeDtypeStruct((t // ATQ, GROUP, ATQ), BF16),
                   jax.ShapeDtypeStruct((t, n - GROUP), F32)],
        compiler_params=_cparams("arbitrary"),
        name="in_proj",
    )(x, mod, g, wqv_t, w_rest)


def _attn_kernel(qt_ref, k_ref, vt_ref, bias_ref, lam_ref, g_ref, o_ref, qz_ref, acc_ref, m_ref, *, lam_init):
    i = pl.program_id(1)
    qt = qt_ref[...]
    rowi = lax.broadcasted_iota(I32, (GROUP, ATQ), 0)
    for c in range(N_MAPS):
        lo = c * ATT_QK
        qz_ref[c] = jnp.where((rowi >= lo) & (rowi < lo + ATT_QK), qt, jnp.zeros_like(qt))
    ones = jnp.ones((ONES_ROWS, ATQ), BF16)

    def step(jb, bias_idx, init):
        kb = k_ref[pl.ds(pl.multiple_of(jb * ATQ, ATQ), ATQ), :]
        vb = vt_ref[jb]
        m_old = None if init else m_ref[...]
        acc_old = None if init else [acc_ref[c] for c in range(N_MAPS)]
        scores = [jnp.dot(kb, qz_ref[c], preferred_element_type=F32) for c in range(N_MAPS)]
        m_new, probs, alphas = [], [], []
        for c in range(N_MAPS):
            s = scores[c]
            if bias_idx is not None:
                s = s + bias_ref[c // 2, bias_idx]
            mn = jnp.max(s, axis=0, keepdims=True)
            if not init:
                mo = m_old[c:c + 1, :]
                mn = jnp.maximum(mo, mn)
                alphas.append(jnp.exp(mo - mn))
            probs.append(jnp.exp(s - mn).astype(BF16))
            m_new.append(mn)
        acc_new = []
        for c in range(N_MAPS):
            h = c // 2
            vaug = jnp.concatenate([vb[h * ATT_V:(h + 1) * ATT_V, :], ones], axis=0)
            pv = jnp.dot(vaug, probs[c], preferred_element_type=F32)
            acc_new.append(pv if init else alphas[c] * acc_old[c] + pv)
        m_ref[...] = jnp.concatenate(m_new, axis=0)
        for c in range(N_MAPS):
            acc_ref[c] = acc_new[c]

    step(i, 1, True)

    @pl.when(i >= 1)
    def _():
        step(i - 1, 0, False)

    def far(j, carry):
        step(j, None, False)
        return carry

    lax.fori_loop(0, jnp.maximum(i - 1, 0), far, 0)

    lp = lam_ref[...]
    lam = (jnp.exp(jnp.sum(lp[0:1] * lp[1:2], axis=1, keepdims=True))
           - jnp.exp(jnp.sum(lp[2:3] * lp[3:4], axis=1, keepdims=True)) + lam_init)
    for h in range(ATT_HEADS):
        a0 = acc_ref[2 * h]
        a1 = acc_ref[2 * h + 1]
        o = a0[0:ATT_V] / a0[ATT_V:ATT_V + 1] - lam * (a1[0:ATT_V] / a1[ATT_V:ATT_V + 1])
        ms = jnp.mean(o * o, axis=0, keepdims=True)
        y = o * lax.rsqrt(ms + EPS) * g_ref[...] * (1.0 - lam_init)
        o_ref[h * ATT_V:(h + 1) * ATT_V, :] = y.astype(o_ref.dtype)


def _attn_call(qt, k, vt, bias, lam_par, subln_col, batch, seq, lam_init):
    t = k.shape[0]
    nq = seq // ATQ
    return pl.pallas_call(
        functools.partial(_attn_kernel, lam_init=lam_init),
        grid=(batch, nq),
        in_specs=[pl.BlockSpec((GROUP, ATQ), lambda b, i: (0, b * nq + i)),
                  pl.BlockSpec((seq, GROUP), lambda b, i: (b, 0)),
                  pl.BlockSpec((nq, GROUP, ATQ), lambda b, i: (b, 0, 0)),
                  pl.BlockSpec((ATT_HEADS, 2, ATQ, ATQ), lambda b, i: (0, 0, 0, 0)),
                  pl.BlockSpec((4, ATT_QK), lambda b, i: (0, 0)),
                  pl.BlockSpec((ATT_V, 1), lambda b, i: (0, 0))],
        out_specs=pl.BlockSpec((GROUP, ATQ), lambda b, i: (0, b * nq + i)),
        out_shape=jax.ShapeDtypeStruct((GROUP, t), BF16),
        scratch_shapes=[pltpu.VMEM((N_MAPS, GROUP, ATQ), BF16),
                        pltpu.VMEM((N_MAPS, ATT_V + ONES_ROWS, ATQ), F32),
                        pltpu.VMEM((N_MAPS, ATQ), F32)],
        compiler_params=_cparams("arbitrary", "arbitrary"),
        name="diff_attn",
    )(qt, k, vt, bias, lam_par, subln_col)


def _rel_bucket_np(n):
    max_exact = NUM_BUCKETS // 2
    n = np.maximum(n, 0)
    ratio = (np.log(np.maximum(n, 1).astype(np.float32) / np.float32(max_exact))
             / np.float32(math.log(MAX_DISTANCE / max_exact)))
    large = np.minimum(max_exact + (ratio * (NUM_BUCKETS - max_exact)).astype(np.int32), NUM_BUCKETS - 1)
    return np.where(n < max_exact, n, large).astype(np.int32)


def _bias_kernel(bucket_ref, rb_ref, o_ref):
    h = pl.program_id(0)
    far = rb_ref[NUM_BUCKETS - 1, h]
    for tile in range(2):
        b = bucket_ref[tile]
        acc = jnp.full(b.shape, NEG_INF, F32)
        for k in range(NUM_BUCKETS):
            acc = jnp.where(b == k, rb_ref[k, h] - far, acc)
        o_ref[0, tile] = acc


def _attn_bias_tiles(rel_bias):
    rel_cur = np.arange(ATQ)[None, :] - np.arange(ATQ)[:, None]
    assert np.all(_rel_bucket_np(np.arange(ATQ + 1, 8 * ATQ)) == NUM_BUCKETS - 1)
    buckets = np.stack([_rel_bucket_np(rel_cur + ATQ),
                        np.where(rel_cur >= 0, _rel_bucket_np(rel_cur), NUM_BUCKETS)]).astype(np.int32)
    return pl.pallas_call(
        _bias_kernel,
        grid=(ATT_HEADS,),
        in_specs=[pl.BlockSpec((2, ATQ, ATQ), lambda h: (0, 0, 0)),
                  pl.BlockSpec(memory_space=pltpu.SMEM)],
        out_specs=pl.BlockSpec((1, 2, ATQ, ATQ), lambda h: (h, 0, 0, 0)),
        out_shape=jax.ShapeDtypeStruct((ATT_HEADS, 2, ATQ, ATQ), F32),
        compiler_params=_cparams("arbitrary"),
        name="attn_bias",
    )(jnp.asarray(buckets), rel_bias)


CONV_HALO = 32
SCONV_HALO = 8


def _layer_norm(x, g, b):
    mu = jnp.mean(x, axis=-1, keepdims=True)
    xc = x - mu
    var = jnp.mean(xc * xc, axis=-1, keepdims=True)
    return xc * lax.rsqrt(var + EPS) * g + b


def _conv_kernel(cur_ref, hga_ref, hgb_ref, hcc_ref, hch_ref, dw_ref, lng_ref, lnb_ref, sw_ref,
                 sg_ref, sb_ref, ws_ref, bs_ref, o_ref, ext_ref, ext2_ref):
    i = pl.program_id(1)
    ts = cur_ref.shape[0]
    first = i == 0

    def col(k):
        return cur_ref[:, k * GROUP:(k + 1) * GROUP]

    zh = hga_ref[...] * jax.nn.sigmoid(hgb_ref[...])
    ext_ref[0:CONV_HALO, :] = jnp.where(first, 0.0, zh)
    ext_ref[CONV_HALO:, :] = col(0) * jax.nn.sigmoid(col(1))
    acc = jnp.zeros((ts, GROUP), F32)
    for j in range(CONF_W):
        off = CONV_HALO - (CONF_W - 1) + j
        acc = acc + dw_ref[j:j + 1, :] * ext_ref[off:off + ts, :]
    zb = _layer_norm(acc, lng_ref[...], lnb_ref[...])
    o_ref[:, 0:GROUP] = (zb * jax.nn.sigmoid(zb)).astype(o_ref.dtype)

    ext2_ref[0:SCONV_HALO, :] = jnp.where(first, 0.0, hcc_ref[...] * hch_ref[...])
    ext2_ref[SCONV_HALO:, :] = col(3) * col(4)
    acc = jnp.zeros((ts, GROUP), F32)
    for j in range(SCONV_W):
        off = SCONV_HALO - (SCONV_W - 1) + j
        acc = acc + sw_ref[j:j + 1, :] * ext2_ref[off:off + ts, :]
    o_ref[:, GROUP:2 * GROUP] = (col(2) * acc).astype(o_ref.dtype)

    lane = lax.broadcasted_iota(I32, (1, GROUP), 1)
    rr = lax.broadcasted_iota(I32, (SGU_CHUNK, SGU_CHUNK), 0)
    cc = lax.broadcasted_iota(I32, (SGU_CHUNK, SGU_CHUNK), 1)
    hd = GROUP // SGU_HEADS
    w_tril = [jnp.where(rr >= cc, ws_ref[h], 0.0).astype(BF16) for h in range(SGU_HEADS)]
    for c in range(ts // SGU_CHUNK):
        rows = slice(c * SGU_CHUNK, (c + 1) * SGU_CHUNK)
        u = _gelu(cur_ref[rows, 5 * GROUP:6 * GROUP])
        v = _layer_norm(_gelu(cur_ref[rows, 6 * GROUP:7 * GROUP]), sg_ref[...], sb_ref[...]).astype(BF16)
        z = bs_ref[...]
        for h in range(SGU_HEADS):
            zh_ = jnp.dot(w_tril[h], v, preferred_element_type=F32)
            z = z + jnp.where((lane >= h * hd) & (lane < (h + 1) * hd), zh_, 0.0)
        o_ref[rows, 2 * GROUP:3 * GROUP] = (u * z).astype(o_ref.dtype)


def _conv_call(pc, conf_dw, conf_ln_g, conf_ln_b, sconv_w, sgu_ln_g, sgu_ln_b, sgu_w, sgu_bias_tile,
               batch, seq):
    t, n = pc.shape
    ts = 512
    nst = seq // ts

    def halo_map(rows, colblk):
        per = ts // rows
        return lambda b, i: (jnp.maximum((b * nst + i) * per - 1, 0), colblk)

    vec = lambda: pl.BlockSpec((1, GROUP), lambda b, i: (0, 0))
    return pl.pallas_call(
        _conv_kernel,
        grid=(batch, nst),
        in_specs=[pl.BlockSpec((ts, n), lambda b, i: (b * nst + i, 0)),
                  pl.BlockSpec((CONV_HALO, GROUP), halo_map(CONV_HALO, 0)),
                  pl.BlockSpec((CONV_HALO, GROUP), halo_map(CONV_HALO, 1)),
                  pl.BlockSpec((SCONV_HALO, GROUP), halo_map(SCONV_HALO, 3)),
                  pl.BlockSpec((SCONV_HALO, GROUP), halo_map(SCONV_HALO, 4)),
                  pl.BlockSpec((CONF_W, GROUP), lambda b, i: (0, 0)),
                  vec(), vec(),
                  pl.BlockSpec((SCONV_W, GROUP), lambda b, i: (0, 0)),
                  vec(), vec(),
                  pl.BlockSpec((SGU_HEADS, SGU_CHUNK, SGU_CHUNK), lambda b, i: (0, 0, 0)),
                  pl.BlockSpec((SGU_CHUNK, GROUP), lambda b, i: (0, 0))],
        out_specs=pl.BlockSpec((ts, 3 * GROUP), lambda b, i: (b * nst + i, 0)),
        out_shape=jax.ShapeDtypeStruct((t, 3 * GROUP), BF16),
        scratch_shapes=[pltpu.VMEM((CONV_HALO + ts, GROUP), F32),
                        pltpu.VMEM((SCONV_HALO + ts, GROUP), F32)],
        compiler_params=_cparams("arbitrary", "arbitrary"),
        name="conv_mix",
    )(pc, pc, pc, pc, pc, conf_dw, conf_ln_g, conf_ln_b, sconv_w, sgu_ln_g, sgu_ln_b, sgu_w,
      sgu_bias_tile)


def _extract_top(s, rowi, n_rounds, want_rank):
    sentinel = float(s.shape[0])
    vals = []
    mark = jnp.full(s.shape, float(n_rounds) if want_rank else 0.0, F32)
    for r in range(n_rounds):
        m = jnp.max(s, axis=0, keepdims=True)
        idx = jnp.min(jnp.where(s == m, rowi, sentinel), axis=0, keepdims=True)
        hit = rowi == idx
        mark = jnp.where(hit, float(r) if want_rank else 1.0, mark)
        s = jnp.where(hit, -jnp.inf, s)
        vals.append(m)
    return vals, mark


def _batcher_network(n):
    pairs = []

    def merge(lo, m, r):
        step = 2 * r
        if step < m:
            merge(lo, m, step)
            merge(lo + r, m, step)
            pairs.extend((i, i + r) for i in range(lo + r, lo + m - r, step))
        else:
            pairs.append((lo, lo + r))

    def sort(lo, m):
        if m > 1:
            sort(lo, m // 2)
            sort(lo + m // 2, m // 2)
            merge(lo, m, 1)

    sort(0, n)
    return pairs


def _compare_exchange(blocks, i, j):
    hi, lo = jnp.maximum(blocks[i], blocks[j]), jnp.minimum(blocks[i], blocks[j])
    blocks[i], blocks[j] = hi, lo


def _bitonic_merge(blocks):
    n = len(blocks)
    d = n // 2
    while d >= 1:
        for i in range(n):
            if i & d == 0:
                _compare_exchange(blocks, i, i + d)
        d //= 2
    return blocks


def _sorted_top(s):
    n = s.shape[0] // SUBLANES
    assert n == TOPK
    blocks = [s[j * SUBLANES:(j + 1) * SUBLANES, :] for j in range(n)]
    for i, j in _batcher_network(n):
        _compare_exchange(blocks, i, j)
    shift = SUBLANES // 2
    while shift >= 1:
        rolled = [pltpu.roll(b, shift, 0) for b in blocks]
        blocks = _bitonic_merge([jnp.maximum(blocks[j], rolled[n - 1 - j]) for j in range(n)])
        shift //= 2
    return blocks


def _extract_values(s, n_rounds):
    vals = []
    for _ in range(n_rounds):
        m = jnp.max(s, axis=0, keepdims=True)
        s = jnp.where(s == m, -jnp.inf, s)
        vals.append(m)
    return vals


def _tile_rows(block, n_rows):
    return jnp.concatenate([block] * (n_rows // block.shape[0]), axis=0)


def _pack_pair(x):
    bits = pltpu.bitcast(x.astype(BF16).astype(F32), jnp.uint32)
    return bits | (bits >> 16)


def _outtop_kernel(x_ref, yat_ref, yb_ref, mod_ref, wo_ref, g_ref, wq_ref, keys_ref,
                   xo_ref, h2t_ref, c1_ref, e1_ref, r2_ref, e2_ref, qs_ref):
    d = x_ref.shape[1]
    tm = x_ref.shape[0]
    y = lax.dot_general(yat_ref[...], wo_ref[0:GROUP, :], _TN, preferred_element_type=F32)
    y = y + jnp.dot(yb_ref[...], wo_ref[GROUP:, :], preferred_element_type=F32)
    x = x_ref[...] + mod_ref[0, :, 2 * d:3 * d] * y
    xo_ref[...] = x
    h2f = _modulated_norm(x, g_ref[...], mod_ref[0, :, 3 * d:4 * d], mod_ref[0, :, 4 * d:5 * d])
    h2 = h2f.astype(BF16)
    h2t_ref[...] = _as_words(jnp.transpose(h2f).astype(BF16))
    q = jnp.dot(h2, wq_ref[...], preferred_element_type=F32).astype(BF16)
    for hp in range(2 * PEER_HEADS):
        qs_ref[hp] = q[:, hp * NKEYS:(hp + 1) * NKEYS]

    rowi = lax.broadcasted_iota(I32, (NKEYS, tm), 0).astype(F32)
    rowc = lax.broadcasted_iota(I32, (N_CAND_PAD, tm), 0)
    rowc_f = rowc.astype(F32)
    a_row = jnp.zeros((N_CAND_PAD, tm), I32) - 1
    for st in CAND_STARTS:
        a_row = a_row + (rowc >= st).astype(I32)
    a_row = jnp.where(rowc >= N_CAND, TOPK, a_row)
    start_row = jnp.zeros((N_CAND_PAD, tm), I32)
    for a, st in enumerate(CAND_STARTS):
        start_row = jnp.where(a_row == a, st, start_row)
    b_row = rowc - start_row

    def candidates(v1, v2):
        rep1 = jnp.full((N_CAND_PAD, tm), -jnp.inf, F32)
        rep2 = jnp.full((N_CAND_PAD, tm), -jnp.inf, F32)
        for a in range(TOPK):
            rep1 = jnp.where(a_row == a, v1[a], rep1)
            rep2 = jnp.where(b_row == a, v2[a], rep2)
        return rep1 + rep2

    def group_counts(sel):
        return [jnp.sum(sel[st:st + n], axis=0, keepdims=True) for st, n in zip(CAND_STARTS, CAND_COUNTS)]

    def store(hh, s1, s2, max1, max2, z, c1, r2):
        c1_ref[hh] = _pack_pair(c1)
        e1_ref[hh] = _pack_pair(jnp.exp(s1 - max1) / z)
        r2_ref[hh] = _as_words(r2.astype(BF16))
        e2_ref[hh] = _as_words(jnp.exp(s2 - max2).astype(BF16))

    def head(hh, carry):
        s1, s2 = [lax.dot_general(keys_ref[2 * hh + p], qs_ref[2 * hh + p], _NT,
                                  preferred_element_type=F32) for p in range(2)]

        v1 = _sorted_top(s1)
        v2 = _sorted_top(s2)
        cand = candidates([v[0:1] for v in v1], [v[0:1] for v in v2])
        top = _extract_values(cand, TOPK)
        cmax, thr = top[0], top[TOPK - 1]
        z = jnp.zeros_like(cmax)
        for val in top:
            z = z + jnp.exp(val - cmax)
        counts = group_counts((cand >= thr).astype(F32))
        c1 = jnp.zeros((NKEYS, tm), F32)
        r2 = jnp.full((NKEYS, tm), float(TOPK), F32)
        for a in reversed(range(TOPK)):
            c1 = jnp.where(s1 == _tile_rows(v1[a], NKEYS), counts[a], c1)
            r2 = jnp.where(s2 == _tile_rows(v2[a], NKEYS), float(a), r2)
        store(hh, s1, s2, v1[0][0:1], v2[0][0:1], z, c1, r2)

        tied = jnp.sum(c1, axis=0, keepdims=True) != float(TOPK)
        for s, v in ((s1, v1), (s2, v2)):
            dup = v[0] == v[1]
            for a in range(1, TOPK - 1):
                dup = dup | (v[a] == v[a + 1])
            n_top = jnp.sum((s >= _tile_rows(v[TOPK - 1], NKEYS)).astype(F32), axis=0, keepdims=True)
            tied = tied | (n_top != float(TOPK)) | (jnp.max(dup.astype(F32), axis=0, keepdims=True) > 0.0)

        @pl.when(jnp.max(tied.astype(F32)) > 0.0)
        def _():
            w1, rank1 = _extract_top(s1, rowi, TOPK, True)
            w2, rank2 = _extract_top(s2, rowi, TOPK, True)
            cand_x = candidates(w1, w2)
            _, sel = _extract_top(cand_x, rowc_f, TOPK, False)
            cmax_x = w1[0] + w2[0]
            z_x = jnp.sum(sel * jnp.exp(jnp.where(sel > 0.0, cand_x, cmax_x) - cmax_x), axis=0, keepdims=True)
            counts_x = group_counts(sel)
            c1_x = jnp.zeros((NKEYS, tm), F32)
            for a in range(TOPK):
                c1_x = jnp.where(rank1 == float(a), counts_x[a], c1_x)
            store(hh, s1, s2, w1[0], w2[0], z_x, c1_x, rank2)

        return carry

    lax.fori_loop(0, PEER_HEADS, head, 0)


def _outtop_call(x, yat, ybcd, mod, w_out, g2, wq, keys, seq):
    t, d = x.shape
    tm = 256
    nq = wq.shape[1]
    tab = lambda: pl.BlockSpec((PEER_HEADS, NKEYS, tm), lambda i: (0, 0, i))
    half = lambda: pl.BlockSpec((PEER_HEADS, NKEYS // 2, tm), lambda i: (0, 0, i))
    pair_shape = jax.ShapeDtypeStruct((PEER_HEADS, NKEYS, t), jnp.uint32)
    half_shape = jax.ShapeDtypeStruct((PEER_HEADS, NKEYS // 2, t), jnp.uint32)
    return pl.pallas_call(
        _outtop_kernel,
        grid=(t // tm,),
        in_specs=[pl.BlockSpec((tm, d), lambda i: (i, 0)),
                  pl.BlockSpec((GROUP, tm), lambda i: (0, i)),
                  pl.BlockSpec((tm, 3 * GROUP), lambda i: (i, 0)),
                  pl.BlockSpec((1, 1, N_MOD * d), lambda i: ((i * tm) // seq, 0, 0)),
                  pl.BlockSpec((4 * GROUP, d), lambda i: (0, 0)),
                  pl.BlockSpec((1, d), lambda i: (0, 0)),
                  pl.BlockSpec((d, nq), lambda i: (0, 0)),
                  pl.BlockSpec((2 * PEER_HEADS, NKEYS, NKEYS), lambda i: (0, 0, 0))],
        out_specs=[pl.BlockSpec((tm, d), lambda i: (i, 0)),
                   pl.BlockSpec((d // 2, tm), lambda i: (0, i)),
                   tab(), tab(), half(), half()],
        out_shape=[jax.ShapeDtypeStruct((t, d), F32),
                   jax.ShapeDtypeStruct((d // 2, t), jnp.uint32),
                   pair_shape, pair_shape, half_shape, half_shape],
        scratch_shapes=[pltpu.VMEM((2 * PEER_HEADS, tm, NKEYS), BF16)],
        compiler_params=_cparams("arbitrary"),
        name="out_topk",
    )(x, yat, ybcd, mod, w_out, g2, wq, keys)


def _bcast_pair_rows(row_u32, n_rows):
    tile = pltpu.bitcast(jnp.broadcast_to(row_u32, (SUBLANES, row_u32.shape[1])), BF16)
    return jnp.concatenate([tile] * (n_rows // (2 * SUBLANES)), axis=0)


def _peer_kernel(h2t_ref, u_ref, x_ref, mod_ref, c1_ref, e1_ref, r2_ref, e2_ref, vt_ref, fg_ref,
                 o_ref, acc_ref, a0_ref, a1_ref, *, nk, final):
    s = pl.program_id(0)
    kp = jnp.maximum(s - 1, 0) % nk
    nb = vt_ref.shape[1] // NKEYS
    d = x_ref.shape[1]

    @pl.when(s == 0)
    def _():
        a1_ref[...] = jnp.zeros_like(a1_ref)

    @pl.when(kp == 0)
    def _():
        acc_ref[...] = jnp.zeros_like(acc_ref)

    n_col = x_ref.shape[0] // PEER_TN

    def body(a_cur, a_prev):
        def cols(j):
            return slice(j * PEER_TN, (j + 1) * PEER_TN)

        def first_matmul(j):
            a_cur[:, cols(j)] = jnp.dot(_as_bf16(u_ref[...]), _as_bf16(h2t_ref[:, cols(j)]),
                                        preferred_element_type=F32)

        def gated_act(j):
            blocks = []
            for ii in range(nb):
                act = _gelu(a_prev[ii * NKEYS:(ii + 1) * NKEYS, cols(j)].astype(BF16))
                gate = None
                for hh in range(PEER_HEADS):
                    c1 = _bcast_pair_rows(c1_ref[hh, kp, ii:ii + 1, cols(j)], NKEYS)
                    e1 = _bcast_pair_rows(e1_ref[hh, kp, ii:ii + 1, cols(j)], NKEYS)
                    g = jnp.where(_as_bf16(r2_ref[hh, :, cols(j)]) < c1, _as_bf16(e2_ref[hh, :, cols(j)]) * e1,
                                  jnp.zeros_like(e1))
                    gate = g if gate is None else gate + g
                blocks.append(gate * act)
            return jnp.concatenate(blocks, axis=0)

        def second_matmul(j, w):
            acc_ref[:, cols(j)] += jnp.dot(_as_bf16(vt_ref[...]), w, preferred_element_type=F32)

        first_matmul(0)
        for j in range(n_col):
            if j + 1 < n_col:
                first_matmul(j + 1)
            second_matmul(j, gated_act(j))

    @pl.when(s % 2 == 0)
    def _():
        body(a0_ref, a1_ref)

    @pl.when(s % 2 == 1)
    def _():
        body(a1_ref, a0_ref)

    @pl.when((kp == nk - 1) & (s >= 1))
    def _():
        y = jnp.transpose(acc_ref[...])
        x = x_ref[...] + mod_ref[0, :, 5 * d:6 * d] * y
        if final:
            ms = jnp.mean(x * x, axis=-1, keepdims=True)
            x = x * lax.rsqrt(ms + EPS) * fg_ref[...]
        o_ref[...] = x


def _expert_tables_kernel(u_ref, v_ref, uo_ref, vo_ref):
    uo_ref[...] = _as_words(u_ref[...].astype(BF16))
    vo_ref[...] = _as_words(jnp.transpose(v_ref[...]).astype(BF16))


def _expert_tables_call(peer_u, peer_v):
    depth, ne, d = peer_u.shape
    ec = PEER_EC
    return pl.pallas_call(
        _expert_tables_kernel,
        grid=(depth, ne // ec),
        in_specs=[pl.BlockSpec((None, ec, d), lambda l, k: (l, k, 0)),
                  pl.BlockSpec((None, ec, d), lambda l, k: (l, k, 0))],
        out_specs=[pl.BlockSpec((None, ec // 2, d), lambda l, k: (l, k, 0)),
                   pl.BlockSpec((None, None, d // 2, ec), lambda l, k: (l, k, 0, 0))],
        out_shape=[jax.ShapeDtypeStruct((depth, ne // 2, d), jnp.uint32),
                   jax.ShapeDtypeStruct((depth, ne // ec, d // 2, ec), jnp.uint32)],
        compiler_params=_cparams("arbitrary", "arbitrary"),
        name="expert_tables",
    )(peer_u, peer_v)


def _peer_call(x, h2, mod, c1, e1, r2, e2, u, vt, final_g, seq, final):
    t, d = x.shape
    nk, _, ec = vt.shape
    tm = PEER_TM
    n_i = t // tm
    last = n_i * nk - 1
    cur_i = lambda s: jnp.minimum(s, last) // nk
    cur_k = lambda s: jnp.minimum(s, last) % nk
    prev_i = lambda s: jnp.maximum(s - 1, 0) // nk
    prev_k = lambda s: jnp.maximum(s - 1, 0) % nk
    once = pl.Buffered(1)
    tab = lambda: pl.BlockSpec((PEER_HEADS, NKEYS // 2, tm), lambda s: (0, 0, prev_i(s)), pipeline_mode=once)
    rows = lambda: pl.BlockSpec((PEER_HEADS, nk, NKEYS // nk, tm), lambda s: (0, 0, 0, prev_i(s)),
                                pipeline_mode=once)
    c1 = c1.reshape(PEER_HEADS, nk, NKEYS // nk, t)
    e1 = e1.reshape(PEER_HEADS, nk, NKEYS // nk, t)
    return pl.pallas_call(
        functools.partial(_peer_kernel, nk=nk, final=final),
        grid=(n_i * nk + 1,),
        in_specs=[pl.BlockSpec((d // 2, tm), lambda s: (0, cur_i(s))),
                  pl.BlockSpec((ec // 2, d), lambda s: (cur_k(s), 0)),
                  pl.BlockSpec((tm, d), lambda s: (prev_i(s), 0), pipeline_mode=once),
                  pl.BlockSpec((1, 1, N_MOD * d), lambda s: ((prev_i(s) * tm) // seq, 0, 0)),
                  rows(), rows(), tab(), tab(),
                  pl.BlockSpec((None, d // 2, ec), lambda s: (prev_k(s), 0, 0)),
                  pl.BlockSpec((1, d), lambda s: (0, 0))],
        out_specs=pl.BlockSpec((tm, d), lambda s: (prev_i(s), 0), pipeline_mode=once),
        out_shape=jax.ShapeDtypeStruct((t, d), F32),
        scratch_shapes=[pltpu.VMEM((d, tm), F32),
                        pltpu.VMEM((ec, tm), F32),
                        pltpu.VMEM((ec, tm), F32)],
        compiler_params=_cparams("arbitrary"),
        name="peer_dense",
    )(h2, u, x, mod, c1, e1, r2, e2, vt, final_g)


def kernel(x, c, rel_bias, w_mod, b_mod, norm1_g, norm2_g, w_in, w_out, diff_lambda, subln_g, conf_dw, conf_ln_g, conf_ln_b, sconv_w, sgu_ln_g, sgu_ln_b, sgu_w, sgu_b, peer_wq, peer_keys, peer_u, peer_v, final_g):
    batch, seq, d = x.shape
    depth = w_in.shape[0]
    t = batch * seq
    assert w_in.shape[2] == N_SLICES * GROUP and seq % PEER_TM == 0 and d % LANES == 0

    xf = x.reshape(t, d)
    mods = _mod_call(c, w_mod, b_mod).reshape(depth, batch, 1, N_MOD * d)
    attn_bias = _attn_bias_tiles(rel_bias)
    u_words, vt_words = _expert_tables_call(peer_u, peer_v)
    row = lambda v: v.reshape(1, -1)

    for l in range(depth):
        lam_init = 0.8 - 0.6 * math.exp(-0.3 * l)
        wl = w_in[l]
        wqv_t = jnp.transpose(jnp.concatenate([wl[:, 0:GROUP], wl[:, 2 * GROUP:3 * GROUP]], axis=1)).astype(BF16)
        w_rest = jnp.concatenate([wl[:, GROUP:2 * GROUP], wl[:, 3 * GROUP:]], axis=1).astype(BF16)
        qt, k, vt, pc = _in_call(xf, mods[l], row(norm1_g[l]), wqv_t, w_rest, seq)
        yat = _attn_call(qt, k, vt, attn_bias, diff_lambda[l], subln_g[l].reshape(ATT_V, 1),
                         batch, seq, lam_init)
        sgu_bias_tile = jnp.repeat(jnp.transpose(sgu_b[l]), GROUP // SGU_HEADS, axis=1)
        ybcd = _conv_call(pc, conf_dw[l], row(conf_ln_g[l]), row(conf_ln_b[l]), sconv_w[l],
                          row(sgu_ln_g[l]), row(sgu_ln_b[l]), sgu_w[l], sgu_bias_tile, batch, seq)
        keys = peer_keys[l].reshape(2 * PEER_HEADS, NKEYS, -1).astype(BF16)
        xf, h2, c1, e1, r2, e2 = _outtop_call(xf, yat, ybcd, mods[l], w_out[l].astype(BF16),
                                              row(norm2_g[l]), peer_wq[l].astype(BF16), keys, seq)
        xf = _peer_call(xf, h2, mods[l], c1, e1, r2, e2, u_words[l], vt_words[l], row(final_g),
                        seq, l == depth - 1)
    return xf.reshape(batch, seq, d)
```

```python
import functools
import math

import jax
import jax.numpy as jnp
import numpy as np
from jax import lax
from jax.experimental import pallas as pl
from jax.experimental.pallas import tpu as pltpu

F32 = jnp.float32
BF16 = jnp.bfloat16
I32 = jnp.int32
HIGHEST = lax.Precision.HIGHEST

GROUP = 256
N_SLICES = 10
ATT_HEADS = 4
ATT_V = 64
ATT_QK = 32
N_MAPS = 2 * ATT_HEADS
NUM_BUCKETS = 32
MAX_DISTANCE = 128
CONF_W = 31
SCONV_W = 3
SGU_HEADS = 4
SGU_CHUNK = 128
PEER_HEADS = 8
NKEYS = 128
TOPK = 16
N_MOD = 6
EPS = 1e-6
NEG_INF = -1e30
SQRT_HALF = float(np.sqrt(0.5))

LANES = 128
SUBLANES = 8
VMEM_LIMIT_BYTES = 56 * 1024 * 1024

PEER_TM = 1024
PEER_EC = 1024
PEER_TN = 256
ATQ = 256
ONES_ROWS = 16

CAND_COUNTS = [TOPK // (a + 1) for a in range(TOPK)]
CAND_STARTS = [int(v) for v in np.cumsum([0] + CAND_COUNTS[:-1])]
N_CAND = sum(CAND_COUNTS)
N_CAND_PAD = -(-N_CAND // SUBLANES) * SUBLANES

_NT = (((1,), (1,)), ((), ()))
_TN = (((0,), (0,)), ((), ()))


def _as_words(x):
    return pltpu.bitcast(x, jnp.uint32)


def _as_bf16(words):
    return pltpu.bitcast(words, BF16)


def _gelu(x):
    return 0.5 * x * (1.0 + lax.erf(x * SQRT_HALF))


def _cparams(*sem):
    return pltpu.CompilerParams(dimension_semantics=sem, vmem_limit_bytes=VMEM_LIMIT_BYTES)


def _mod_kernel(c_ref, w_ref, b_ref, o_ref):
    c = c_ref[...]
    cond = c * jax.nn.sigmoid(c)
    o_ref[0] = jnp.dot(cond.astype(BF16), w_ref[0].astype(BF16), preferred_element_type=F32) + b_ref[0]


def _mod_call(c, w_mod, b_mod):
    depth, d, n = w_mod.shape
    b = c.shape[0]
    tn = 2048
    return pl.pallas_call(
        _mod_kernel,
        grid=(depth, n // tn),
        in_specs=[pl.BlockSpec((b, d), lambda l, j: (0, 0)),
                  pl.BlockSpec((1, d, tn), lambda l, j: (l, 0, j)),
                  pl.BlockSpec((1, 1, tn), lambda l, j: (l, 0, j))],
        out_specs=pl.BlockSpec((1, b, tn), lambda l, j: (l, 0, j)),
        out_shape=jax.ShapeDtypeStruct((depth, b, n), F32),
        compiler_params=_cparams("arbitrary", "arbitrary"),
        name="mod",
    )(c, w_mod, b_mod.reshape(depth, 1, n))


def _modulated_norm(x, g, shift, scale):
    ms = jnp.mean(x * x, axis=-1, keepdims=True)
    return (x * lax.rsqrt(ms + EPS) * g) * (1.0 + scale) + shift


def _in_kernel(x_ref, mod_ref, g_ref, wqv_ref, wr_ref, qt_ref, k_ref, vt_ref, pc_ref):
    d = x_ref.shape[1]
    h = _modulated_norm(x_ref[...], g_ref[...], mod_ref[0, :, 0:d], mod_ref[0, :, d:2 * d]).astype(BF16)
    qv = lax.dot_general(wqv_ref[...], h, _NT, preferred_element_type=F32)
    qt_ref[...] = (qv[0:GROUP] * (ATT_QK ** -0.5)).astype(BF16)
    for c in range(vt_ref.shape[0]):
        vt_ref[c] = qv[GROUP:2 * GROUP, c * ATQ:(c + 1) * ATQ].astype(BF16)
    res = jnp.dot(h, wr_ref[...], preferred_element_type=F32)
    k_ref[...] = res[:, 0:GROUP].astype(BF16)
    pc_ref[...] = res[:, GROUP:]


def _in_call(x, mod, g, wqv_t, w_rest, seq):
    t, d = x.shape
    n = w_rest.shape[1]
    tm = 512
    return pl.pallas_call(
        _in_kernel,
        grid=(t // tm,),
        in_specs=[pl.BlockSpec((tm, d), lambda i: (i, 0)),
                  pl.BlockSpec((1, 1, N_MOD * d), lambda i: ((i * tm) // seq, 0, 0)),
                  pl.BlockSpec((1, d), lambda i: (0, 0)),
                  pl.BlockSpec((2 * GROUP, d), lambda i: (0, 0)),
                  pl.BlockSpec((d, n), lambda i: (0, 0))],
        out_specs=[pl.BlockSpec((GROUP, tm), lambda i: (0, i)),
                   pl.BlockSpec((tm, GROUP), lambda i: (i, 0)),
                   pl.BlockSpec((tm // ATQ, GROUP, ATQ), lambda i: (i, 0, 0)),
                   pl.BlockSpec((tm, n - GROUP), lambda i: (i, 0))],
        out_shape=[jax.ShapeDtypeStruct((GROUP, t), BF16),
                   jax.ShapeDtypeStruct((t, GROUP), BF16),
                   jax.ShapeDtypeStruct((t // ATQ, GROUP, ATQ), BF16),
                   jax.ShapeDtypeStruct((t, n - GROUP), F32)],
        compiler_params=_cparams("arbitrary"),
        name="in_proj",
    )(x, mod, g, wqv_t, w_rest)


def _attn_kernel(qt_ref, k_ref, vt_ref, bias_ref, lam_ref, g_ref, o_ref, qz_ref, acc_ref, m_ref, *, lam_init):
    i = pl.program_id(1)
    qt = qt_ref[...]
    rowi = lax.broadcasted_iota(I32, (GROUP, ATQ), 0)
    for c in range(N_MAPS):
        lo = c * ATT_QK
        qz_ref[c] = jnp.where((rowi >= lo) & (rowi < lo + ATT_QK), qt, jnp.zeros_like(qt))
    ones = jnp.ones((ONES_ROWS, ATQ), BF16)

    def step(jb, bias_idx, init):
        kb = k_ref[pl.ds(pl.multiple_of(jb * ATQ, ATQ), ATQ), :]
        vb = vt_ref[jb]
        m_old = None if init else m_ref[...]
        acc_old = None if init else [acc_ref[c] for c in range(N_MAPS)]
        scores = [jnp.dot(kb, qz_ref[c], preferred_element_type=F32) for c in range(N_MAPS)]
        m_new, probs, alphas = [], [], []
        for c in range(N_MAPS):
            s = scores[c]
            if bias_idx is not None:
                s = s + bias_ref[c // 2, bias_idx]
            mn = jnp.max(s, axis=0, keepdims=True)
            if not init:
                mo = m_old[c:c + 1, :]
                mn = jnp.maximum(mo, mn)
                alphas.append(jnp.exp(mo - mn))
            probs.append(jnp.exp(s - mn).astype(BF16))
            m_new.append(mn)
        acc_new = []
        for c in range(N_MAPS):
            h = c // 2
            vaug = jnp.concatenate([vb[h * ATT_V:(h + 1) * ATT_V, :], ones], axis=0)
            pv = jnp.dot(vaug, probs[c], preferred_element_type=F32)
            acc_new.append(pv if init else alphas[c] * acc_old[c] + pv)
        m_ref[...] = jnp.concatenate(m_new, axis=0)
        for c in range(N_MAPS):
            acc_ref[c] = acc_new[c]

    step(i, 1, True)

    @pl.when(i >= 1)
    def _():
        step(i - 1, 0, False)

    def far(j, carry):
        step(j, None, False)
        return carry

    lax.fori_loop(0, jnp.maximum(i - 1, 0), far, 0)

    lp = lam_ref[...]
    lam = (jnp.exp(jnp.sum(lp[0:1] * lp[1:2], axis=1, keepdims=True))
           - jnp.exp(jnp.sum(lp[2:3] * lp[3:4], axis=1, keepdims=True)) + lam_init)
    for h in range(ATT_HEADS):
        a0 = acc_ref[2 * h]
        a1 = acc_ref[2 * h + 1]
        o = a0[0:ATT_V] / a0[ATT_V:ATT_V + 1] - lam * (a1[0:ATT_V] / a1[ATT_V:ATT_V + 1])
        ms = jnp.mean(o * o, axis=0, keepdims=True)
        y = o * lax.rsqrt(ms + EPS) * g_ref[...] * (1.0 - lam_init)
        o_ref[h * ATT_V:(h + 1) * ATT_V, :] = y.astype(o_ref.dtype)


def _attn_call(qt, k, vt, bias, lam_par, subln_col, batch, seq, lam_init):
    t = k.shape[0]
    nq = seq // ATQ
    return pl.pallas_call(
        functools.partial(_attn_kernel, lam_init=lam_init),
        grid=(batch, nq),
        in_specs=[pl.BlockSpec((GROUP, ATQ), lambda b, i: (0, b * nq + i)),
                  pl.BlockSpec((seq, GROUP), lambda b, i: (b, 0)),
                  pl.BlockSpec((nq, GROUP, ATQ), lambda b, i: (b, 0, 0)),
                  pl.BlockSpec((ATT_HEADS, 2, ATQ, ATQ), lambda b, i: (0, 0, 0, 0)),
                  pl.BlockSpec((4, ATT_QK), lambda b, i: (0, 0)),
                  pl.BlockSpec((ATT_V, 1), lambda b, i: (0, 0))],
        out_specs=pl.BlockSpec((GROUP, ATQ), lambda b, i: (0, b * nq + i)),
        out_shape=jax.ShapeDtypeStruct((GROUP, t), BF16),
        scratch_shapes=[pltpu.VMEM((N_MAPS, GROUP, ATQ), BF16),
                        pltpu.VMEM((N_MAPS, ATT_V + ONES_ROWS, ATQ), F32),
                        pltpu.VMEM((N_MAPS, ATQ), F32)],
        compiler_params=_cparams("arbitrary", "arbitrary"),
        name="diff_attn",
    )(qt, k, vt, bias, lam_par, subln_col)


def _rel_bucket_np(n):
    max_exact = NUM_BUCKETS // 2
    n = np.maximum(n, 0)
    ratio = (np.log(np.maximum(n, 1).astype(np.float32) / np.float32(max_exact))
             / np.float32(math.log(MAX_DISTANCE / max_exact)))
    large = np.minimum(max_exact + (ratio * (NUM_BUCKETS - max_exact)).astype(np.int32), NUM_BUCKETS - 1)
    return np.where(n < max_exact, n, large).astype(np.int32)


def _bias_kernel(bucket_ref, rb_ref, o_ref):
    h = pl.program_id(0)
    far = rb_ref[NUM_BUCKETS - 1, h]
    for tile in range(2):
        b = bucket_ref[tile]
        acc = jnp.full(b.shape, NEG_INF, F32)
        for k in range(NUM_BUCKETS):
            acc = jnp.where(b == k, rb_ref[k, h] - far, acc)
        o_ref[0, tile] = acc


def _attn_bias_tiles(rel_bias):
    rel_cur = np.arange(ATQ)[None, :] - np.arange(ATQ)[:, None]
    assert np.all(_rel_bucket_np(np.arange(ATQ + 1, 8 * ATQ)) == NUM_BUCKETS - 1)
    buckets = np.stack([_rel_bucket_np(rel_cur + ATQ),
                        np.where(rel_cur >= 0, _rel_bucket_np(rel_cur), NUM_BUCKETS)]).astype(np.int32)
    return pl.pallas_call(
        _bias_kernel,
        grid=(ATT_HEADS,),
        in_specs=[pl.BlockSpec((2, ATQ, ATQ), lambda h: (0, 0, 0)),
                  pl.BlockSpec(memory_space=pltpu.SMEM)],
        out_specs=pl.BlockSpec((1, 2, ATQ, ATQ), lambda h: (h, 0, 0, 0)),
        out_shape=jax.ShapeDtypeStruct((ATT_HEADS, 2, ATQ, ATQ), F32),
        compiler_params=_cparams("arbitrary"),
        name="attn_bias",
    )(jnp.asarray(buckets), rel_bias)


CONV_HALO = 32
SCONV_HALO = 8


def _layer_norm(x, g, b):
    mu = jnp.mean(x, axis=-1, keepdims=True)
    xc = x - mu
    var = jnp.mean(xc * xc, axis=-1, keepdims=True)
    return xc * lax.rsqrt(var + EPS) * g + b


def _conv_kernel(cur_ref, hga_ref, hgb_ref, hcc_ref, hch_ref, dw_ref, lng_ref, lnb_ref, sw_ref,
                 sg_ref, sb_ref, ws_ref, bs_ref, o_ref, ext_ref, ext2_ref):
    i = pl.program_id(1)
    ts = cur_ref.shape[0]
    first = i == 0

    def col(k):
        return cur_ref[:, k * GROUP:(k + 1) * GROUP]

    zh = hga_ref[...] * jax.nn.sigmoid(hgb_ref[...])
    ext_ref[0:CONV_HALO, :] = jnp.where(first, 0.0, zh)
    ext_ref[CONV_HALO:, :] = col(0) * jax.nn.sigmoid(col(1))
    acc = jnp.zeros((ts, GROUP), F32)
    for j in range(CONF_W):
        off = CONV_HALO - (CONF_W - 1) + j
        acc = acc + dw_ref[j:j + 1, :] * ext_ref[off:off + ts, :]
    zb = _layer_norm(acc, lng_ref[...], lnb_ref[...])
    o_ref[:, 0:GROUP] = (zb * jax.nn.sigmoid(zb)).astype(o_ref.dtype)

    ext2_ref[0:SCONV_HALO, :] = jnp.where(first, 0.0, hcc_ref[...] * hch_ref[...])
    ext2_ref[SCONV_HALO:, :] = col(3) * col(4)
    acc = jnp.zeros((ts, GROUP), F32)
    for j in range(SCONV_W):
        off = SCONV_HALO - (SCONV_W - 1) + j
        acc = acc + sw_ref[j:j + 1, :] * ext2_ref[off:off + ts, :]
    o_ref[:, GROUP:2 * GROUP] = (col(2) * acc).astype(o_ref.dtype)

    lane = lax.broadcasted_iota(I32, (1, GROUP), 1)
    rr = lax.broadcasted_iota(I32, (SGU_CHUNK, SGU_CHUNK), 0)
    cc = lax.broadcasted_iota(I32, (SGU_CHUNK, SGU_CHUNK), 1)
    hd = GROUP // SGU_HEADS
    w_tril = [jnp.where(rr >= cc, ws_ref[h], 0.0).astype(BF16) for h in range(SGU_HEADS)]
    for c in range(ts // SGU_CHUNK):
        rows = slice(c * SGU_CHUNK, (c + 1) * SGU_CHUNK)
        u = _gelu(cur_ref[rows, 5 * GROUP:6 * GROUP])
        v = _layer_norm(_gelu(cur_ref[rows, 6 * GROUP:7 * GROUP]), sg_ref[...], sb_ref[...]).astype(BF16)
        z = bs_ref[...]
        for h in range(SGU_HEADS):
            zh_ = jnp.dot(w_tril[h], v, preferred_element_type=F32)
            z = z + jnp.where((lane >= h * hd) & (lane < (h + 1) * hd), zh_, 0.0)
        o_ref[rows, 2 * GROUP:3 * GROUP] = (u * z).astype(o_ref.dtype)


def _conv_call(pc, conf_dw, conf_ln_g, conf_ln_b, sconv_w, sgu_ln_g, sgu_ln_b, sgu_w, sgu_bias_tile,
               batch, seq):
    t, n = pc.shape
    ts = 512
    nst = seq // ts

    def halo_map(rows, colblk):
        per = ts // rows
        return lambda b, i: (jnp.maximum((b * nst + i) * per - 1, 0), colblk)

    vec = lambda: pl.BlockSpec((1, GROUP), lambda b, i: (0, 0))
    return pl.pallas_call(
        _conv_kernel,
        grid=(batch, nst),
        in_specs=[pl.BlockSpec((ts, n), lambda b, i: (b * nst + i, 0)),
                  pl.BlockSpec((CONV_HALO, GROUP), halo_map(CONV_HALO, 0)),
                  pl.BlockSpec((CONV_HALO, GROUP), halo_map(CONV_HALO, 1)),
                  pl.BlockSpec((SCONV_HALO, GROUP), halo_map(SCONV_HALO, 3)),
                  pl.BlockSpec((SCONV_HALO, GROUP), halo_map(SCONV_HALO, 4)),
                  pl.BlockSpec((CONF_W, GROUP), lambda b, i: (0, 0)),
                  vec(), vec(),
                  pl.BlockSpec((SCONV_W, GROUP), lambda b, i: (0, 0)),
                  vec(), vec(),
                  pl.BlockSpec((SGU_HEADS, SGU_CHUNK, SGU_CHUNK), lambda b, i: (0, 0, 0)),
                  pl.BlockSpec((SGU_CHUNK, GROUP), lambda b, i: (0, 0))],
        out_specs=pl.BlockSpec((ts, 3 * GROUP), lambda b, i: (b * nst + i, 0)),
        out_shape=jax.ShapeDtypeStruct((t, 3 * GROUP), BF16),
        scratch_shapes=[pltpu.VMEM((CONV_HALO + ts, GROUP), F32),
                        pltpu.VMEM((SCONV_HALO + ts, GROUP), F32)],
        compiler_params=_cparams("arbitrary", "arbitrary"),
        name="conv_mix",
    )(pc, pc, pc, pc, pc, conf_dw, conf_ln_g, conf_ln_b, sconv_w, sgu_ln_g, sgu_ln_b, sgu_w,
      sgu_bias_tile)


def _extract_top(s, rowi, n_rounds, want_rank):
    sentinel = float(s.shape[0])
    vals = []
    mark = jnp.full(s.shape, float(n_rounds) if want_rank else 0.0, F32)
    for r in range(n_rounds):
        m = jnp.max(s, axis=0, keepdims=True)
        idx = jnp.min(jnp.where(s == m, rowi, sentinel), axis=0, keepdims=True)
        hit = rowi == idx
        mark = jnp.where(hit, float(r) if want_rank else 1.0, mark)
        s = jnp.where(hit, -jnp.inf, s)
        vals.append(m)
    return vals, mark


def _batcher_network(n):
    pairs = []

    def merge(lo, m, r):
        step = 2 * r
        if step < m:
            merge(lo, m, step)
            merge(lo + r, m, step)
            pairs.extend((i, i + r) for i in range(lo + r, lo + m - r, step))
        else:
            pairs.append((lo, lo + r))

    def sort(lo, m):
        if m > 1:
            sort(lo, m // 2)
            sort(lo + m // 2, m // 2)
            merge(lo, m, 1)

    sort(0, n)
    return pairs


def _compare_exchange(blocks, i, j):
    hi, lo = jnp.maximum(blocks[i], blocks[j]), jnp.minimum(blocks[i], blocks[j])
    blocks[i], blocks[j] = hi, lo


def _bitonic_merge(blocks):
    n = len(blocks)
    d = n // 2
    while d >= 1:
        for i in range(n):
            if i & d == 0:
                _compare_exchange(blocks, i, i + d)
        d //= 2
    return blocks


def _sorted_top(s):
    n = s.shape[0] // SUBLANES
    assert n == TOPK
    blocks = [s[j * SUBLANES:(j + 1) * SUBLANES, :] for j in range(n)]
    for i, j in _batcher_network(n):
        _compare_exchange(blocks, i, j)
    shift = SUBLANES // 2
    while shift >= 1:
        rolled = [pltpu.roll(b, shift, 0) for b in blocks]
        blocks = _bitonic_merge([jnp.maximum(blocks[j], rolled[n - 1 - j]) for j in range(n)])
        shift //= 2
    return blocks


def _extract_values(s, n_rounds):
    vals = []
    for _ in range(n_rounds):
        m = jnp.max(s, axis=0, keepdims=True)
        s = jnp.where(s == m, -jnp.inf, s)
        vals.append(m)
    return vals


def _tile_rows(block, n_rows):
    return jnp.concatenate([block] * (n_rows // block.shape[0]), axis=0)


def _pack_pair(x):
    bits = pltpu.bitcast(x.astype(BF16).astype(F32), jnp.uint32)
    return bits | (bits >> 16)


def _outtop_kernel(x_ref, yat_ref, yb_ref, mod_ref, wo_ref, g_ref, wq_ref, keys_ref,
                   xo_ref, h2t_ref, c1_ref, e1_ref, r2_ref, e2_ref, qs_ref):
    d = x_ref.shape[1]
    tm = x_ref.shape[0]
    y = lax.dot_general(yat_ref[...], wo_ref[0:GROUP, :], _TN, preferred_element_type=F32)
    y = y + jnp.dot(yb_ref[...], wo_ref[GROUP:, :], preferred_element_type=F32)
    x = x_ref[...] + mod_ref[0, :, 2 * d:3 * d] * y
    xo_ref[...] = x
    h2f = _modulated_norm(x, g_ref[...], mod_ref[0, :, 3 * d:4 * d], mod_ref[0, :, 4 * d:5 * d])
    h2 = h2f.astype(BF16)
    h2t_ref[...] = _as_words(jnp.transpose(h2f).astype(BF16))
    q = jnp.dot(h2, wq_ref[...], preferred_element_type=F32).astype(BF16)
    for hp in range(2 * PEER_HEADS):
        qs_ref[hp] = q[:, hp * NKEYS:(hp + 1) * NKEYS]

    rowi = lax.broadcasted_iota(I32, (NKEYS, tm), 0).astype(F32)
    rowc = lax.broadcasted_iota(I32, (N_CAND_PAD, tm), 0)
    rowc_f = rowc.astype(F32)
    a_row = jnp.zeros((N_CAND_PAD, tm), I32) - 1
    for st in CAND_STARTS:
        a_row = a_row + (rowc >= st).astype(I32)
    a_row = jnp.where(rowc >= N_CAND, TOPK, a_row)
    start_row = jnp.zeros((N_CAND_PAD, tm), I32)
    for a, st in enumerate(CAND_STARTS):
        start_row = jnp.where(a_row == a, st, start_row)
    b_row = rowc - start_row

    def candidates(v1, v2):
        rep1 = jnp.full((N_CAND_PAD, tm), -jnp.inf, F32)
        rep2 = jnp.full((N_CAND_PAD, tm), -jnp.inf, F32)
        for a in range(TOPK):
            rep1 = jnp.where(a_row == a, v1[a], rep1)
            rep2 = jnp.where(b_row == a, v2[a], rep2)
        return rep1 + rep2

    def group_counts(sel):
        return [jnp.sum(sel[st:st + n], axis=0, keepdims=True) for st, n in zip(CAND_STARTS, CAND_COUNTS)]

    def store(hh, s1, s2, max1, max2, z, c1, r2):
        c1_ref[hh] = _pack_pair(c1)
        e1_ref[hh] = _pack_pair(jnp.exp(s1 - max1) / z)
        r2_ref[hh] = _as_words(r2.astype(BF16))
        e2_ref[hh] = _as_words(jnp.exp(s2 - max2).astype(BF16))

    def head(hh, carry):
        s1, s2 = [lax.dot_general(keys_ref[2 * hh + p], qs_ref[2 * hh + p], _NT,
                                  preferred_element_type=F32) for p in range(2)]

        v1 = _sorted_top(s1)
        v2 = _sorted_top(s2)
        cand = candidates([v[0:1] for v in v1], [v[0:1] for v in v2])
        top = _extract_values(cand, TOPK)
        cmax, thr = top[0], top[TOPK - 1]
        z = jnp.zeros_like(cmax)
        for val in top:
            z = z + jnp.exp(val - cmax)
        counts = group_counts((cand >= thr).astype(F32))
        c1 = jnp.zeros((NKEYS, tm), F32)
        r2 = jnp.full((NKEYS, tm), float(TOPK), F32)
        for a in reversed(range(TOPK)):
            c1 = jnp.where(s1 == _tile_rows(v1[a], NKEYS), counts[a], c1)
            r2 = jnp.where(s2 == _tile_rows(v2[a], NKEYS), float(a), r2)
        store(hh, s1, s2, v1[0][0:1], v2[0][0:1], z, c1, r2)

        tied = jnp.sum(c1, axis=0, keepdims=True) != float(TOPK)
        for s, v in ((s1, v1), (s2, v2)):
            dup = v[0] == v[1]
            for a in range(1, TOPK - 1):
                dup = dup | (v[a] == v[a + 1])
            n_top = jnp.sum((s >= _tile_rows(v[TOPK - 1], NKEYS)).astype(F32), axis=0, keepdims=True)
            tied = tied | (n_top != float(TOPK)) | (jnp.max(dup.astype(F32), axis=0, keepdims=True) > 0.0)

        @pl.when(jnp.max(tied.astype(F32)) > 0.0)
        def _():
            w1, rank1 = _extract_top(s1, rowi, TOPK, True)
            w2, rank2 = _extract_top(s2, rowi, TOPK, True)
            cand_x = candidates(w1, w2)
            _, sel = _extract_top(cand_x, rowc_f, TOPK, False)
            cmax_x = w1[0] + w2[0]
            z_x = jnp.sum(sel * jnp.exp(jnp.where(sel > 0.0, cand_x, cmax_x) - cmax_x), axis=0, keepdims=True)
            counts_x = group_counts(sel)
            c1_x = jnp.zeros((NKEYS, tm), F32)
            for a in range(TOPK):
                c1_x = jnp.where(rank1 == float(a), counts_x[a], c1_x)
            store(hh, s1, s2, w1[0], w2[0], z_x, c1_x, rank2)

        return carry

    lax.fori_loop(0, PEER_HEADS, head, 0)


def _outtop_call(x, yat, ybcd, mod, w_out, g2, wq, keys, seq):
    t, d = x.shape
    tm = 256
    nq = wq.shape[1]
    tab = lambda: pl.BlockSpec((PEER_HEADS, NKEYS, tm), lambda i: (0, 0, i))
    half = lambda: pl.BlockSpec((PEER_HEADS, NKEYS // 2, tm), lambda i: (0, 0, i))
    pair_shape = jax.ShapeDtypeStruct((PEER_HEADS, NKEYS, t), jnp.uint32)
    half_shape = jax.ShapeDtypeStruct((PEER_HEADS, NKEYS // 2, t), jnp.uint32)
    return pl.pallas_call(
        _outtop_kernel,
        grid=(t // tm,),
        in_specs=[pl.BlockSpec((tm, d), lambda i: (i, 0)),
                  pl.BlockSpec((GROUP, tm), lambda i: (0, i)),
                  pl.BlockSpec((tm, 3 * GROUP), lambda i: (i, 0)),
                  pl.BlockSpec((1, 1, N_MOD * d), lambda i: ((i * tm) // seq, 0, 0)),
                  pl.BlockSpec((4 * GROUP, d), lambda i: (0, 0)),
                  pl.BlockSpec((1, d), lambda i: (0, 0)),
                  pl.BlockSpec((d, nq), lambda i: (0, 0)),
                  pl.BlockSpec((2 * PEER_HEADS, NKEYS, NKEYS), lambda i: (0, 0, 0))],
        out_specs=[pl.BlockSpec((tm, d), lambda i: (i, 0)),
                   pl.BlockSpec((d // 2, tm), lambda i: (0, i)),
                   tab(), tab(), half(), half()],
        out_shape=[jax.ShapeDtypeStruct((t, d), F32),
                   jax.ShapeDtypeStruct((d // 2, t), jnp.uint32),
                   pair_shape, pair_shape, half_shape, half_shape],
        scratch_shapes=[pltpu.VMEM((2 * PEER_HEADS, tm, NKEYS), BF16)],
        compiler_params=_cparams("arbitrary"),
        name="out_topk",
    )(x, yat, ybcd, mod, w_out, g2, wq, keys)


def _bcast_pair_rows(row_u32, n_rows):
    tile = pltpu.bitcast(jnp.broadcast_to(row_u32, (SUBLANES, row_u32.shape[1])), BF16)
    return jnp.concatenate([tile] * (n_rows // (2 * SUBLANES)), axis=0)


def _peer_kernel(h2t_ref, u_ref, x_ref, mod_ref, c1_ref, e1_ref, r2_ref, e2_ref, vt_ref, fg_ref,
                 o_ref, acc_ref, a0_ref, a1_ref, *, nk, final):
    s = pl.program_id(0)
    kp = jnp.maximum(s - 1, 0) % nk
    nb = vt_ref.shape[1] // NKEYS
    d = x_ref.shape[1]

    @pl.when(s == 0)
    def _():
        a1_ref[...] = jnp.zeros_like(a1_ref)

    @pl.when(kp == 0)
    def _():
        acc_ref[...] = jnp.zeros_like(acc_ref)

    n_col = x_ref.shape[0] // PEER_TN

    def body(a_cur, a_prev):
        def cols(j):
            return slice(j * PEER_TN, (j + 1) * PEER_TN)

        def first_matmul(j):
            a_cur[:, cols(j)] = jnp.dot(_as_bf16(u_ref[...]), _as_bf16(h2t_ref[:, cols(j)]),
                                        preferred_element_type=F32).astype(BF16)

        def gated_act(j):
            blocks = []
            for ii in range(nb):
                act = _gelu(a_prev[ii * NKEYS:(ii + 1) * NKEYS, cols(j)])
                gate = None
                for hh in range(PEER_HEADS):
                    c1 = _bcast_pair_rows(c1_ref[hh, kp, ii:ii + 1, cols(j)], NKEYS)
                    e1 = _bcast_pair_rows(e1_ref[hh, kp, ii:ii + 1, cols(j)], NKEYS)
                    g = jnp.where(_as_bf16(r2_ref[hh, :, cols(j)]) < c1, _as_bf16(e2_ref[hh, :, cols(j)]) * e1,
                                  jnp.zeros_like(e1))
                    gate = g if gate is None else gate + g
                blocks.append(gate * act)
            return jnp.concatenate(blocks, axis=0)

        def second_matmul(j, w):
            acc_ref[:, cols(j)] += jnp.dot(_as_bf16(vt_ref[...]), w, preferred_element_type=F32)

        first_matmul(0)
        for j in range(n_col):
            if j + 1 < n_col:
                first_matmul(j + 1)
            second_matmul(j, gated_act(j))

    @pl.when(s % 2 == 0)
    def _():
        body(a0_ref, a1_ref)

    @pl.when(s % 2 == 1)
    def _():
        body(a1_ref, a0_ref)

    @pl.when((kp == nk - 1) & (s >= 1))
    def _():
        y = jnp.transpose(acc_ref[...])
        x = x_ref[...] + mod_ref[0, :, 5 * d:6 * d] * y
        if final:
            ms = jnp.mean(x * x, axis=-1, keepdims=True)
            x = x * lax.rsqrt(ms + EPS) * fg_ref[...]
        o_ref[...] = x


def _expert_tables_kernel(u_ref, v_ref, uo_ref, vo_ref):
    uo_ref[...] = _as_words(u_ref[...].astype(BF16))
    vo_ref[...] = _as_words(jnp.transpose(v_ref[...]).astype(BF16))


def _expert_tables_call(peer_u, peer_v):
    depth, ne, d = peer_u.shape
    ec = PEER_EC
    return pl.pallas_call(
        _expert_tables_kernel,
        grid=(depth, ne // ec),
        in_specs=[pl.BlockSpec((None, ec, d), lambda l, k: (l, k, 0)),
                  pl.BlockSpec((None, ec, d), lambda l, k: (l, k, 0))],
        out_specs=[pl.BlockSpec((None, ec // 2, d), lambda l, k: (l, k, 0)),
                   pl.BlockSpec((None, None, d // 2, ec), lambda l, k: (l, k, 0, 0))],
        out_shape=[jax.ShapeDtypeStruct((depth, ne // 2, d), jnp.uint32),
                   jax.ShapeDtypeStruct((depth, ne // ec, d // 2, ec), jnp.uint32)],
        compiler_params=_cparams("arbitrary", "arbitrary"),
        name="expert_tables",
    )(peer_u, peer_v)


def _peer_call(x, h2, mod, c1, e1, r2, e2, u, vt, layer, final_g, seq, final):
    t, d = x.shape
    _, nk, _, ec = vt.shape
    tm = PEER_TM
    n_i = t // tm
    last = n_i * nk - 1
    cur_i = lambda s: jnp.minimum(s, last) // nk
    cur_k = lambda s: jnp.minimum(s, last) % nk
    prev_i = lambda s: jnp.maximum(s - 1, 0) // nk
    prev_k = lambda s: jnp.maximum(s - 1, 0) % nk
    once = pl.Buffered(1)
    tab = lambda: pl.BlockSpec((PEER_HEADS, NKEYS // 2, tm), lambda s: (0, 0, prev_i(s)))
    rows = lambda: pl.BlockSpec((PEER_HEADS, nk, NKEYS // nk, tm), lambda s: (0, 0, 0, prev_i(s)),
                                pipeline_mode=once)
    c1 = c1.reshape(PEER_HEADS, nk, NKEYS // nk, t)
    e1 = e1.reshape(PEER_HEADS, nk, NKEYS // nk, t)
    return pl.pallas_call(
        functools.partial(_peer_kernel, nk=nk, final=final),
        grid=(n_i * nk + 1,),
        in_specs=[pl.BlockSpec((d // 2, tm), lambda s: (0, cur_i(s))),
                  pl.BlockSpec((None, ec // 2, d), lambda s: (layer, cur_k(s), 0)),
                  pl.BlockSpec((tm, d), lambda s: (prev_i(s), 0)),
                  pl.BlockSpec((1, 1, N_MOD * d), lambda s: ((prev_i(s) * tm) // seq, 0, 0)),
                  rows(), rows(), tab(), tab(),
                  pl.BlockSpec((None, None, d // 2, ec), lambda s: (layer, prev_k(s), 0, 0)),
                  pl.BlockSpec((1, d), lambda s: (0, 0))],
        out_specs=pl.BlockSpec((tm, d), lambda s: (prev_i(s), 0), pipeline_mode=once),
        out_shape=jax.ShapeDtypeStruct((t, d), F32),
        scratch_shapes=[pltpu.VMEM((d, tm), F32),
                        pltpu.VMEM((ec, tm), BF16),
                        pltpu.VMEM((ec, tm), BF16)],
        compiler_params=_cparams("arbitrary"),
        name="peer_dense",
    )(h2, u, x, mod, c1, e1, r2, e2, vt, final_g)


def kernel(x, c, rel_bias, w_mod, b_mod, norm1_g, norm2_g, w_in, w_out, diff_lambda, subln_g, conf_dw, conf_ln_g, conf_ln_b, sconv_w, sgu_ln_g, sgu_ln_b, sgu_w, sgu_b, peer_wq, peer_keys, peer_u, peer_v, final_g):
    batch, seq, d = x.shape
    depth = w_in.shape[0]
    t = batch * seq
    assert w_in.shape[2] == N_SLICES * GROUP and seq % PEER_TM == 0 and d % LANES == 0

    xf = x.reshape(t, d)
    mods = _mod_call(c, w_mod, b_mod).reshape(depth, batch, 1, N_MOD * d)
    attn_bias = _attn_bias_tiles(rel_bias)
    u_words, vt_words = _expert_tables_call(peer_u, peer_v)
    row = lambda v: v.reshape(1, -1)

    for l in range(depth):
        lam_init = 0.8 - 0.6 * math.exp(-0.3 * l)
        wl = w_in[l]
        wqv_t = jnp.transpose(jnp.concatenate([wl[:, 0:GROUP], wl[:, 2 * GROUP:3 * GROUP]], axis=1)).astype(BF16)
        w_rest = jnp.concatenate([wl[:, GROUP:2 * GROUP], wl[:, 3 * GROUP:]], axis=1).astype(BF16)
        qt, k, vt, pc = _in_call(xf, mods[l], row(norm1_g[l]), wqv_t, w_rest, seq)
        yat = _attn_call(qt, k, vt, attn_bias, diff_lambda[l], subln_g[l].reshape(ATT_V, 1),
                         batch, seq, lam_init)
        sgu_bias_tile = jnp.repeat(jnp.transpose(sgu_b[l]), GROUP // SGU_HEADS, axis=1)
        ybcd = _conv_call(pc, conf_dw[l], row(conf_ln_g[l]), row(conf_ln_b[l]), sconv_w[l],
                          row(sgu_ln_g[l]), row(sgu_ln_b[l]), sgu_w[l], sgu_bias_tile, batch, seq)
        keys = peer_keys[l].reshape(2 * PEER_HEADS, NKEYS, -1).astype(BF16)
        xf, h2, c1, e1, r2, e2 = _outtop_call(xf, yat, ybcd, mods[l], w_out[l].astype(BF16),
                                              row(norm2_g[l]), peer_wq[l].astype(BF16), keys, seq)
        xf = _peer_call(xf, h2, mods[l], c1, e1, r2, e2, u_words, vt_words, l, row(final_g),
                        seq, l == depth - 1)
    return xf.reshape(batch, seq, d)
```

```python
import functools
import math

import jax
import jax.numpy as jnp
import numpy as np
from jax import lax
from jax.experimental import pallas as pl
from jax.experimental.pallas import tpu as pltpu

F32 = jnp.float32
BF16 = jnp.bfloat16
I32 = jnp.int32
HIGHEST = lax.Precision.HIGHEST

GROUP = 256
N_SLICES = 10
ATT_HEADS = 4
ATT_V = 64
ATT_QK = 32
N_MAPS = 2 * ATT_HEADS
NUM_BUCKETS = 32
MAX_DISTANCE = 128
CONF_W = 31
SCONV_W = 3
SGU_HEADS = 4
SGU_CHUNK = 128
PEER_HEADS = 8
NKEYS = 128
TOPK = 16
N_MOD = 6
EPS = 1e-6
NEG_INF = -1e30
SQRT_HALF = float(np.sqrt(0.5))

LANES = 128
SUBLANES = 8
VMEM_LIMIT_BYTES = 56 * 1024 * 1024

PEER_TM = 1024
PEER_EC = 1024
PEER_TN = 256
ATQ = 256
ONES_ROWS = 16

CAND_COUNTS = [TOPK // (a + 1) for a in range(TOPK)]
CAND_STARTS = [int(v) for v in np.cumsum([0] + CAND_COUNTS[:-1])]
N_CAND = sum(CAND_COUNTS)
N_CAND_PAD = -(-N_CAND // SUBLANES) * SUBLANES
CAND_A = [a for a in range(TOPK) for _ in range(CAND_COUNTS[a])] + [None] * (N_CAND_PAD - N_CAND)
CAND_B = [b for a in range(TOPK) for b in range(CAND_COUNTS[a])] + [None] * (N_CAND_PAD - N_CAND)

_NT = (((1,), (1,)), ((), ()))
_TN = (((0,), (0,)), ((), ()))


def _as_words(x):
    return pltpu.bitcast(x, jnp.uint32)


def _as_bf16(words):
    return pltpu.bitcast(words, BF16)


def _gelu(x):
    return 0.5 * x * (1.0 + lax.erf(x * SQRT_HALF))


def _cparams(*sem):
    return pltpu.CompilerParams(dimension_semantics=sem, vmem_limit_bytes=VMEM_LIMIT_BYTES)


def _mod_kernel(c_ref, w_ref, b_ref, o_ref):
    c = c_ref[...]
    cond = c * jax.nn.sigmoid(c)
    o_ref[0] = jnp.dot(cond.astype(BF16), w_ref[0].astype(BF16), preferred_element_type=F32) + b_ref[0]


def _mod_call(c, w_mod, b_mod):
    depth, d, n = w_mod.shape
    b = c.shape[0]
    tn = 2048
    return pl.pallas_call(
        _mod_kernel,
        grid=(depth, n // tn),
        in_specs=[pl.BlockSpec((b, d), lambda l, j: (0, 0)),
                  pl.BlockSpec((1, d, tn), lambda l, j: (l, 0, j)),
                  pl.BlockSpec((1, 1, tn), lambda l, j: (l, 0, j))],
        out_specs=pl.BlockSpec((1, b, tn), lambda l, j: (l, 0, j)),
        out_shape=jax.ShapeDtypeStruct((depth, b, n), F32),
        compiler_params=_cparams("arbitrary", "arbitrary"),
        name="mod",
    )(c, w_mod, b_mod.reshape(depth, 1, n))


def _modulated_norm(x, g, shift, scale):
    ms = jnp.mean(x * x, axis=-1, keepdims=True)
    return (x * lax.rsqrt(ms + EPS) * g) * (1.0 + scale) + shift


def _in_kernel(x_ref, mod_ref, g_ref, wqv_ref, wr_ref, qt_ref, k_ref, vt_ref, pc_ref):
    d = x_ref.shape[1]
    h = _modulated_norm(x_ref[...], g_ref[...], mod_ref[0, :, 0:d], mod_ref[0, :, d:2 * d]).astype(BF16)
    qv = lax.dot_general(wqv_ref[...], h, _NT, preferred_element_type=F32)
    qt_ref[...] = (qv[0:GROUP] * (ATT_QK ** -0.5)).astype(BF16)
    for c in range(vt_ref.shape[0]):
        vt_ref[c] = qv[GROUP:2 * GROUP, c * ATQ:(c + 1) * ATQ].astype(BF16)
    res = jnp.dot(h, wr_ref[...], preferred_element_type=F32)
    k_ref[...] = res[:, 0:GROUP].astype(BF16)
    pc_ref[...] = res[:, GROUP:]


def _in_call(x, mod, g, wqv_t, w_rest, seq):
    t, d = x.shape
    n = w_rest.shape[1]
    tm = 512
    return pl.pallas_call(
        _in_kernel,
        grid=(t // tm,),
        in_specs=[pl.BlockSpec((tm, d), lambda i: (i, 0)),
                  pl.BlockSpec((1, 1, N_MOD * d), lambda i: ((i * tm) // seq, 0, 0)),
                  pl.BlockSpec((1, d), lambda i: (0, 0)),
                  pl.BlockSpec((2 * GROUP, d), lambda i: (0, 0)),
                  pl.BlockSpec((d, n), lambda i: (0, 0))],
        out_specs=[pl.BlockSpec((GROUP, tm), lambda i: (0, i)),
                   pl.BlockSpec((tm, GROUP), lambda i: (i, 0)),
                   pl.BlockSpec((tm // ATQ, GROUP, ATQ), lambda i: (i, 0, 0)),
                   pl.BlockSpec((tm, n - GROUP), lambda i: (i, 0))],
        out_shape=[jax.ShapeDtypeStruct((GROUP, t), BF16),
                   jax.ShapeDtypeStruct((t, GROUP), BF16),
                   jax.ShapeDtypeStruct((t // ATQ, GROUP, ATQ), BF16),
                   jax.ShapeDtypeStruct((t, n - GROUP), F32)],
        compiler_params=_cparams("arbitrary"),
        name="in_proj",
    )(x, mod, g, wqv_t, w_rest)


def _attn_kernel(qt_ref, k_ref, vt_ref, bias_ref, lam_ref, g_ref, o_ref, qz_ref, acc_ref, m_ref, *, lam_init):
    i = pl.program_id(1)
    qt = qt_ref[...]
    rowi = lax.broadcasted_iota(I32, (GROUP, ATQ), 0)
    for c in range(N_MAPS):
        lo = c * ATT_QK
        qz_ref[c] = jnp.where((rowi >= lo) & (rowi < lo + ATT_QK), qt, jnp.zeros_like(qt))
    ones = jnp.ones((ONES_ROWS, ATQ), BF16)

    def step(jb, bias_idx, init):
        kb = k_ref[pl.ds(pl.multiple_of(jb * ATQ, ATQ), ATQ), :]
        vb = vt_ref[jb]
        m_old = None if init else m_ref[...]
        acc_old = None if init else [acc_ref[c] for c in range(N_MAPS)]
        scores = [jnp.dot(kb, qz_ref[c], preferred_element_type=F32) for c in range(N_MAPS)]
        m_new, probs, alphas = [], [], []
        for c in range(N_MAPS):
            s = scores[c]
            if bias_idx is not None:
                s = s + bias_ref[c // 2, bias_idx]
            mn = jnp.max(s, axis=0, keepdims=True)
            if not init:
                mo = m_old[c:c + 1, :]
                mn = jnp.maximum(mo, mn)
                alphas.append(jnp.exp(mo - mn))
            probs.append(jnp.exp(s - mn).astype(BF16))
            m_new.append(mn)
        acc_new = []
        for c in range(N_MAPS):
            h = c // 2
            vaug = jnp.concatenate([vb[h * ATT_V:(h + 1) * ATT_V, :], ones], axis=0)
            pv = jnp.dot(vaug, probs[c], preferred_element_type=F32)
            acc_new.append(pv if init else alphas[c] * acc_old[c] + pv)
        m_ref[...] = jnp.concatenate(m_new, axis=0)
        for c in range(N_MAPS):
            acc_ref[c] = acc_new[c]

    step(i, 1, True)

    @pl.when(i >= 1)
    def _():
        step(i - 1, 0, False)

    def far(j, carry):
        step(j, None, False)
        return carry

    lax.fori_loop(0, jnp.maximum(i - 1, 0), far, 0)

    lp = lam_ref[...]
    lam = (jnp.exp(jnp.sum(lp[0:1] * lp[1:2], axis=1, keepdims=True))
           - jnp.exp(jnp.sum(lp[2:3] * lp[3:4], axis=1, keepdims=True)) + lam_init)
    for h in range(ATT_HEADS):
        a0 = acc_ref[2 * h]
        a1 = acc_ref[2 * h + 1]
        o = a0[0:ATT_V] / a0[ATT_V:ATT_V + 1] - lam * (a1[0:ATT_V] / a1[ATT_V:ATT_V + 1])
        ms = jnp.mean(o * o, axis=0, keepdims=True)
        y = o * lax.rsqrt(ms + EPS) * g_ref[...] * (1.0 - lam_init)
        o_ref[h * ATT_V:(h + 1) * ATT_V, :] = y.astype(o_ref.dtype)


def _attn_call(qt, k, vt, bias, lam_par, subln_col, batch, seq, lam_init):
    t = k.shape[0]
    nq = seq // ATQ
    return pl.pallas_call(
        functools.partial(_attn_kernel, lam_init=lam_init),
        grid=(batch, nq),
        in_specs=[pl.BlockSpec((GROUP, ATQ), lambda b, i: (0, b * nq + i)),
                  pl.BlockSpec((seq, GROUP), lambda b, i: (b, 0)),
                  pl.BlockSpec((nq, GROUP, ATQ), lambda b, i: (b, 0, 0)),
                  pl.BlockSpec((ATT_HEADS, 2, ATQ, ATQ), lambda b, i: (0, 0, 0, 0)),
                  pl.BlockSpec((4, ATT_QK), lambda b, i: (0, 0)),
                  pl.BlockSpec((ATT_V, 1), lambda b, i: (0, 0))],
        out_specs=pl.BlockSpec((GROUP, ATQ), lambda b, i: (0, b * nq + i)),
        out_shape=jax.ShapeDtypeStruct((GROUP, t), BF16),
        scratch_shapes=[pltpu.VMEM((N_MAPS, GROUP, ATQ), BF16),
                        pltpu.VMEM((N_MAPS, ATT_V + ONES_ROWS, ATQ), F32),
                        pltpu.VMEM((N_MAPS, ATQ), F32)],
        compiler_params=_cparams("arbitrary", "arbitrary"),
        name="diff_attn",
    )(qt, k, vt, bias, lam_par, subln_col)


def _rel_bucket_np(n):
    max_exact = NUM_BUCKETS // 2
    n = np.maximum(n, 0)
    ratio = (np.log(np.maximum(n, 1).astype(np.float32) / np.float32(max_exact))
             / np.float32(math.log(MAX_DISTANCE / max_exact)))
    large = np.minimum(max_exact + (ratio * (NUM_BUCKETS - max_exact)).astype(np.int32), NUM_BUCKETS - 1)
    return np.where(n < max_exact, n, large).astype(np.int32)


def _bias_kernel(bucket_ref, rb_ref, o_ref):
    h = pl.program_id(0)
    far = rb_ref[NUM_BUCKETS - 1, h]
    for tile in range(2):
        b = bucket_ref[tile]
        acc = jnp.full(b.shape, NEG_INF, F32)
        for k in range(NUM_BUCKETS):
            acc = jnp.where(b == k, rb_ref[k, h] - far, acc)
        o_ref[0, tile] = acc


def _attn_bias_tiles(rel_bias):
    rel_cur = np.arange(ATQ)[None, :] - np.arange(ATQ)[:, None]
    assert np.all(_rel_bucket_np(np.arange(ATQ + 1, 8 * ATQ)) == NUM_BUCKETS - 1)
    buckets = np.stack([_rel_bucket_np(rel_cur + ATQ),
                        np.where(rel_cur >= 0, _rel_bucket_np(rel_cur), NUM_BUCKETS)]).astype(np.int32)
    return pl.pallas_call(
        _bias_kernel,
        grid=(ATT_HEADS,),
        in_specs=[pl.BlockSpec((2, ATQ, ATQ), lambda h: (0, 0, 0)),
                  pl.BlockSpec(memory_space=pltpu.SMEM)],
        out_specs=pl.BlockSpec((1, 2, ATQ, ATQ), lambda h: (h, 0, 0, 0)),
        out_shape=jax.ShapeDtypeStruct((ATT_HEADS, 2, ATQ, ATQ), F32),
        compiler_params=_cparams("arbitrary"),
        name="attn_bias",
    )(jnp.asarray(buckets), rel_bias)


CONV_HALO = 32
SCONV_HALO = 8


def _layer_norm(x, g, b):
    mu = jnp.mean(x, axis=-1, keepdims=True)
    xc = x - mu
    var = jnp.mean(xc * xc, axis=-1, keepdims=True)
    return xc * lax.rsqrt(var + EPS) * g + b


def _conv_kernel(cur_ref, hga_ref, hgb_ref, hcc_ref, hch_ref, dw_ref, lng_ref, lnb_ref, sw_ref,
                 sg_ref, sb_ref, ws_ref, bs_ref, o_ref, ext_ref, ext2_ref, shift_ref):
    i = pl.program_id(1)
    ts = cur_ref.shape[0]
    first = i == 0

    def col(k):
        return cur_ref[:, k * GROUP:(k + 1) * GROUP]

    zh = hga_ref[...] * jax.nn.sigmoid(hgb_ref[...])
    ext_ref[0:CONV_HALO, :] = jnp.where(first, 0.0, zh)
    ext_ref[CONV_HALO:, :] = col(0) * jax.nn.sigmoid(col(1))
    acc = jnp.zeros((ts, GROUP), F32)
    first_off = CONV_HALO - (CONF_W - 1)
    for r in range(SUBLANES):
        taps = [j for j in range(CONF_W) if (first_off + j) % SUBLANES == r]
        if r == 0:
            src = ext_ref
        else:
            shift_ref[...] = ext_ref[r:r + shift_ref.shape[0], :]
            src = shift_ref
        for j in taps:
            q = first_off + j - r
            acc = acc + dw_ref[j:j + 1, :] * src[q:q + ts, :]
    zb = _layer_norm(acc, lng_ref[...], lnb_ref[...])
    o_ref[:, 0:GROUP] = (zb * jax.nn.sigmoid(zb)).astype(o_ref.dtype)

    ext2_ref[0:SCONV_HALO, :] = jnp.where(first, 0.0, hcc_ref[...] * hch_ref[...])
    ext2_ref[SCONV_HALO:, :] = col(3) * col(4)
    acc = jnp.zeros((ts, GROUP), F32)
    for j in range(SCONV_W):
        off = SCONV_HALO - (SCONV_W - 1) + j
        acc = acc + sw_ref[j:j + 1, :] * ext2_ref[off:off + ts, :]
    o_ref[:, GROUP:2 * GROUP] = (col(2) * acc).astype(o_ref.dtype)

    lane = lax.broadcasted_iota(I32, (1, GROUP), 1)
    rr = lax.broadcasted_iota(I32, (SGU_CHUNK, SGU_CHUNK), 0)
    cc = lax.broadcasted_iota(I32, (SGU_CHUNK, SGU_CHUNK), 1)
    hd = GROUP // SGU_HEADS
    w_tril = [jnp.where(rr >= cc, ws_ref[h], 0.0).astype(BF16) for h in range(SGU_HEADS)]
    for c in range(ts // SGU_CHUNK):
        rows = slice(c * SGU_CHUNK, (c + 1) * SGU_CHUNK)
        u = _gelu(cur_ref[rows, 5 * GROUP:6 * GROUP])
        v = _layer_norm(_gelu(cur_ref[rows, 6 * GROUP:7 * GROUP]), sg_ref[...], sb_ref[...]).astype(BF16)
        z = bs_ref[...]
        for h in range(SGU_HEADS):
            zh_ = jnp.dot(w_tril[h], v, preferred_element_type=F32)
            z = z + jnp.where((lane >= h * hd) & (lane < (h + 1) * hd), zh_, 0.0)
        o_ref[rows, 2 * GROUP:3 * GROUP] = (u * z).astype(o_ref.dtype)


def _conv_call(pc, conf_dw, conf_ln_g, conf_ln_b, sconv_w, sgu_ln_g, sgu_ln_b, sgu_w, sgu_bias_tile,
               batch, seq):
    t, n = pc.shape
    ts = 512
    nst = seq // ts

    def halo_map(rows, colblk):
        per = ts // rows
        return lambda b, i: (jnp.maximum((b * nst + i) * per - 1, 0), colblk)

    vec = lambda: pl.BlockSpec((1, GROUP), lambda b, i: (0, 0))
    return pl.pallas_call(
        _conv_kernel,
        grid=(batch, nst),
        in_specs=[pl.BlockSpec((ts, n), lambda b, i: (b * nst + i, 0)),
                  pl.BlockSpec((CONV_HALO, GROUP), halo_map(CONV_HALO, 0)),
                  pl.BlockSpec((CONV_HALO, GROUP), halo_map(CONV_HALO, 1)),
                  pl.BlockSpec((SCONV_HALO, GROUP), halo_map(SCONV_HALO, 3)),
                  pl.BlockSpec((SCONV_HALO, GROUP), halo_map(SCONV_HALO, 4)),
                  pl.BlockSpec((CONF_W, GROUP), lambda b, i: (0, 0)),
                  vec(), vec(),
                  pl.BlockSpec((SCONV_W, GROUP), lambda b, i: (0, 0)),
                  vec(), vec(),
                  pl.BlockSpec((SGU_HEADS, SGU_CHUNK, SGU_CHUNK), lambda b, i: (0, 0, 0)),
                  pl.BlockSpec((SGU_CHUNK, GROUP), lambda b, i: (0, 0))],
        out_specs=pl.BlockSpec((ts, 3 * GROUP), lambda b, i: (b * nst + i, 0)),
        out_shape=jax.ShapeDtypeStruct((t, 3 * GROUP), BF16),
        scratch_shapes=[pltpu.VMEM((CONV_HALO + ts, GROUP), F32),
                        pltpu.VMEM((SCONV_HALO + ts, GROUP), F32),
                        pltpu.VMEM((CONV_HALO - SUBLANES + ts, GROUP), F32)],
        compiler_params=_cparams("arbitrary", "arbitrary"),
        name="conv_mix",
    )(pc, pc, pc, pc, pc, conf_dw, conf_ln_g, conf_ln_b, sconv_w, sgu_ln_g, sgu_ln_b, sgu_w,
      sgu_bias_tile)


def _extract_top(s, rowi, n_rounds, want_rank):
    sentinel = float(s.shape[0])
    vals = []
    mark = jnp.full(s.shape, float(n_rounds) if want_rank else 0.0, F32)
    for r in range(n_rounds):
        m = jnp.max(s, axis=0, keepdims=True)
        idx = jnp.min(jnp.where(s == m, rowi, sentinel), axis=0, keepdims=True)
        hit = rowi == idx
        mark = jnp.where(hit, float(r) if want_rank else 1.0, mark)
        s = jnp.where(hit, -jnp.inf, s)
        vals.append(m)
    return vals, mark


def _batcher_network(n):
    pairs = []

    def merge(lo, m, r):
        step = 2 * r
        if step < m:
            merge(lo, m, step)
            merge(lo + r, m, step)
            pairs.extend((i, i + r) for i in range(lo + r, lo + m - r, step))
        else:
            pairs.append((lo, lo + r))

    def sort(lo, m):
        if m > 1:
            sort(lo, m // 2)
            sort(lo + m // 2, m // 2)
            merge(lo, m, 1)

    sort(0, n)
    return pairs


def _compare_exchange(blocks, i, j):
    hi, lo = jnp.maximum(blocks[i], blocks[j]), jnp.minimum(blocks[i], blocks[j])
    blocks[i], blocks[j] = hi, lo


def _bitonic_merge(blocks):
    n = len(blocks)
    d = n // 2
    while d >= 1:
        for i in range(n):
            if i & d == 0:
                _compare_exchange(blocks, i, i + d)
        d //= 2
    return blocks


def _sorted_top(s):
    n = s.shape[0] // SUBLANES
    assert n == TOPK
    blocks = [s[j * SUBLANES:(j + 1) * SUBLANES, :] for j in range(n)]
    for i, j in _batcher_network(n):
        _compare_exchange(blocks, i, j)
    shift = SUBLANES // 2
    while shift >= 1:
        rolled = [pltpu.roll(b, shift, 0) for b in blocks]
        blocks = _bitonic_merge([jnp.maximum(blocks[j], rolled[n - 1 - j]) for j in range(n)])
        shift //= 2
    return blocks


def _extract_values(s, n_rounds):
    vals = []
    for _ in range(n_rounds):
        m = jnp.max(s, axis=0, keepdims=True)
        s = jnp.where(s == m, -jnp.inf, s)
        vals.append(m)
    return vals


def _tile_rows(block, n_rows):
    return jnp.concatenate([block] * (n_rows // block.shape[0]), axis=0)


def _pack_pair(x):
    bits = pltpu.bitcast(x.astype(BF16).astype(F32), jnp.uint32)
    return bits | (bits >> 16)


def _outtop_kernel(x_ref, yat_ref, yb_ref, mod_ref, wo_ref, g_ref, wq_ref, keys_ref,
                   xo_ref, h2t_ref, c1_ref, e1_ref, r2_ref, e2_ref, qs_ref):
    d = x_ref.shape[1]
    tm = x_ref.shape[0]
    y = lax.dot_general(yat_ref[...], wo_ref[0:GROUP, :], _TN, preferred_element_type=F32)
    y = y + jnp.dot(yb_ref[...], wo_ref[GROUP:, :], preferred_element_type=F32)
    x = x_ref[...] + mod_ref[0, :, 2 * d:3 * d] * y
    xo_ref[...] = x
    h2f = _modulated_norm(x, g_ref[...], mod_ref[0, :, 3 * d:4 * d], mod_ref[0, :, 4 * d:5 * d])
    h2 = h2f.astype(BF16)
    h2t_ref[...] = _as_words(jnp.transpose(h2f).astype(BF16))
    q = jnp.dot(h2, wq_ref[...], preferred_element_type=F32).astype(BF16)
    for hp in range(2 * PEER_HEADS):
        qs_ref[hp] = q[:, hp * NKEYS:(hp + 1) * NKEYS]

    tw = tm
    rowi = lax.broadcasted_iota(I32, (NKEYS, tw), 0).astype(F32)
    rowc_f = lax.broadcasted_iota(I32, (N_CAND_PAD, tw), 0).astype(F32)
    sub = lax.broadcasted_iota(I32, (SUBLANES, tw), 0)

    def spread(vals, rank_of_row):
        pick = lambda r: jnp.full((1, tw), -jnp.inf, F32) if r is None else vals[r]
        block = jnp.broadcast_to(pick(rank_of_row[0]), (SUBLANES, tw))
        for p in range(1, SUBLANES):
            if rank_of_row[p] != rank_of_row[p - 1]:
                block = jnp.where(sub >= p, pick(rank_of_row[p]), block)
        return block

    def candidates(v1, v2):
        return jnp.concatenate(
            [spread(v1, CAND_A[k:k + SUBLANES]) + spread(v2, CAND_B[k:k + SUBLANES])
             for k in range(0, N_CAND_PAD, SUBLANES)], axis=0)

    def group_counts(sel):
        return [jnp.sum(sel[st:st + n], axis=0, keepdims=True) for st, n in zip(CAND_STARTS, CAND_COUNTS)]

    def store(hh, lanes, s1, s2, max1, max2, z, c1, r2):
        c1_ref[hh, :, lanes] = _pack_pair(c1)
        e1_ref[hh, :, lanes] = _pack_pair(jnp.exp(s1 - max1) / z)
        r2_ref[hh, :, lanes] = _as_words(r2.astype(BF16))
        e2_ref[hh, :, lanes] = _as_words(jnp.exp(s2 - max2).astype(BF16))

    def chain(hh, lanes):
        s1, s2 = [lax.dot_general(keys_ref[2 * hh + p], qs_ref[2 * hh + p, lanes, :], _NT,
                                  preferred_element_type=F32) for p in range(2)]

        v1 = _sorted_top(s1)
        v2 = _sorted_top(s2)
        cand = candidates(v1, v2)
        top = _extract_values(cand, TOPK)
        cmax, thr = top[0], top[TOPK - 1]
        z = jnp.zeros_like(cmax)
        for val in top:
            z = z + jnp.exp(val - cmax)
        counts = group_counts((cand >= thr).astype(F32))
        c1 = jnp.zeros((NKEYS, tw), F32)
        r2 = jnp.full((NKEYS, tw), float(TOPK), F32)
        for a in reversed(range(TOPK)):
            c1 = jnp.where(s1 == _tile_rows(v1[a], NKEYS), counts[a], c1)
            r2 = jnp.where(s2 == _tile_rows(v2[a], NKEYS), float(a), r2)
        store(hh, lanes, s1, s2, v1[0][0:1], v2[0][0:1], z, c1, r2)

        tied = jnp.sum(c1, axis=0, keepdims=True) != float(TOPK)
        for s, v in ((s1, v1), (s2, v2)):
            dup = v[0] == v[1]
            for a in range(1, TOPK - 1):
                dup = dup | (v[a] == v[a + 1])
            n_top = jnp.sum((s >= _tile_rows(v[TOPK - 1], NKEYS)).astype(F32), axis=0, keepdims=True)
            tied = tied | (n_top != float(TOPK)) | (jnp.max(dup.astype(F32), axis=0, keepdims=True) > 0.0)

        return s1, s2, jnp.max(tied.astype(F32)) > 0.0

    def exact_chain(hh, lanes, s1, s2, tied):
        @pl.when(tied)
        def _():
            w1, rank1 = _extract_top(s1, rowi, TOPK, True)
            w2, rank2 = _extract_top(s2, rowi, TOPK, True)
            cand_x = candidates(w1, w2)
            _, sel = _extract_top(cand_x, rowc_f, TOPK, False)
            cmax_x = w1[0] + w2[0]
            z_x = jnp.sum(sel * jnp.exp(jnp.where(sel > 0.0, cand_x, cmax_x) - cmax_x), axis=0, keepdims=True)
            counts_x = group_counts(sel)
            c1_x = jnp.zeros((NKEYS, tw), F32)
            for a in range(TOPK):
                c1_x = jnp.where(rank1 == float(a), counts_x[a], c1_x)
            store(hh, lanes, s1, s2, w1[0], w2[0], z_x, c1_x, rank2)

    def head(hh, carry):
        tiles = [slice(c * tw, (c + 1) * tw) for c in range(tm // tw)]
        fast = [chain(hh, lanes) for lanes in tiles]
        for lanes, (s1, s2, tied) in zip(tiles, fast):
            exact_chain(hh, lanes, s1, s2, tied)
        return carry

    lax.fori_loop(0, PEER_HEADS, head, 0)


def _outtop_call(x, yat, ybcd, mod, w_out, g2, wq, keys, seq):
    t, d = x.shape
    tm = 256
    nq = wq.shape[1]
    tab = lambda: pl.BlockSpec((PEER_HEADS, NKEYS, tm), lambda i: (0, 0, i))
    half = lambda: pl.BlockSpec((PEER_HEADS, NKEYS // 2, tm), lambda i: (0, 0, i))
    pair_shape = jax.ShapeDtypeStruct((PEER_HEADS, NKEYS, t), jnp.uint32)
    half_shape = jax.ShapeDtypeStruct((PEER_HEADS, NKEYS // 2, t), jnp.uint32)
    return pl.pallas_call(
        _outtop_kernel,
        grid=(t // tm,),
        in_specs=[pl.BlockSpec((tm, d), lambda i: (i, 0)),
                  pl.BlockSpec((GROUP, tm), lambda i: (0, i)),
                  pl.BlockSpec((tm, 3 * GROUP), lambda i: (i, 0)),
                  pl.BlockSpec((1, 1, N_MOD * d), lambda i: ((i * tm) // seq, 0, 0)),
                  pl.BlockSpec((4 * GROUP, d), lambda i: (0, 0)),
                  pl.BlockSpec((1, d), lambda i: (0, 0)),
                  pl.BlockSpec((d, nq), lambda i: (0, 0)),
                  pl.BlockSpec((2 * PEER_HEADS, NKEYS, NKEYS), lambda i: (0, 0, 0))],
        out_specs=[pl.BlockSpec((tm, d), lambda i: (i, 0)),
                   pl.BlockSpec((d // 2, tm), lambda i: (0, i)),
                   tab(), tab(), half(), half()],
        out_shape=[jax.ShapeDtypeStruct((t, d), F32),
                   jax.ShapeDtypeStruct((d // 2, t), jnp.uint32),
                   pair_shape, pair_shape, half_shape, half_shape],
        scratch_shapes=[pltpu.VMEM((2 * PEER_HEADS, tm, NKEYS), BF16)],
        compiler_params=_cparams("arbitrary"),
        name="out_topk",
    )(x, yat, ybcd, mod, w_out, g2, wq, keys)


def _bcast_pair_rows(row_u32, n_rows):
    tile = pltpu.bitcast(jnp.broadcast_to(row_u32, (SUBLANES, row_u32.shape[1])), BF16)
    return jnp.concatenate([tile] * (n_rows // (2 * SUBLANES)), axis=0)


def _peer_kernel(h2t_ref, u_ref, x_ref, mod_ref, c1_ref, e1_ref, r2_ref, e2_ref, vt_ref, fg_ref,
                 o_ref, acc_ref, a0_ref, a1_ref, *, nk, final):
    s = pl.program_id(0)
    kp = jnp.maximum(s - 1, 0) % nk
    nb = vt_ref.shape[1] // NKEYS
    d = x_ref.shape[1]

    @pl.when(s == 0)
    def _():
        a1_ref[...] = jnp.zeros_like(a1_ref)

    @pl.when(kp == 0)
    def _():
        acc_ref[...] = jnp.zeros_like(acc_ref)

    n_col = x_ref.shape[0] // PEER_TN

    def body(a_cur, a_prev):
        def cols(j):
            return slice(j * PEER_TN, (j + 1) * PEER_TN)

        def first_matmul(j):
            a_cur[:, cols(j)] = jnp.dot(_as_bf16(u_ref[...]), _as_bf16(h2t_ref[:, cols(j)]),
                                        preferred_element_type=F32).astype(BF16)

        def gated_act(j):
            blocks = []
            for ii in range(nb):
                act = _gelu(a_prev[ii * NKEYS:(ii + 1) * NKEYS, cols(j)])
                gate = None
                for hh in range(PEER_HEADS):
                    c1 = _bcast_pair_rows(c1_ref[hh, kp, ii:ii + 1, cols(j)], NKEYS)
                    e1 = _bcast_pair_rows(e1_ref[hh, kp, ii:ii + 1, cols(j)], NKEYS)
                    g = jnp.where(_as_bf16(r2_ref[hh, :, cols(j)]) < c1, _as_bf16(e2_ref[hh, :, cols(j)]) * e1,
                                  jnp.zeros_like(e1))
                    gate = g if gate is None else gate + g
                blocks.append(gate * act)
            return jnp.concatenate(blocks, axis=0)

        def second_matmul(j, w):
            acc_ref[:, cols(j)] += jnp.dot(_as_bf16(vt_ref[...]), w, preferred_element_type=F32)

        first_matmul(0)
        for j in range(n_col):
            if j + 1 < n_col:
                first_matmul(j + 1)
            second_matmul(j, gated_act(j))

    @pl.when(s % 2 == 0)
    def _():
        body(a0_ref, a1_ref)

    @pl.when(s % 2 == 1)
    def _():
        body(a1_ref, a0_ref)

    @pl.when((kp == nk - 1) & (s >= 1))
    def _():
        y = jnp.transpose(acc_ref[...])
        x = x_ref[...] + mod_ref[0, :, 5 * d:6 * d] * y
        if final:
            ms = jnp.mean(x * x, axis=-1, keepdims=True)
            x = x * lax.rsqrt(ms + EPS) * fg_ref[...]
        o_ref[...] = x


def _expert_tables_kernel(u_ref, v_ref, uo_ref, vo_ref):
    uo_ref[...] = _as_words(u_ref[...].astype(BF16))
    vo_ref[...] = _as_words(jnp.transpose(v_ref[...]).astype(BF16))


def _expert_tables_call(peer_u, peer_v):
    depth, ne, d = peer_u.shape
    ec = PEER_EC
    return pl.pallas_call(
        _expert_tables_kernel,
        grid=(depth, ne // ec),
        in_specs=[pl.BlockSpec((None, ec, d), lambda l, k: (l, k, 0)),
                  pl.BlockSpec((None, ec, d), lambda l, k: (l, k, 0))],
        out_specs=[pl.BlockSpec((None, ec // 2, d), lambda l, k: (l, k, 0)),
                   pl.BlockSpec((None, None, d // 2, ec), lambda l, k: (l, k, 0, 0))],
        out_shape=[jax.ShapeDtypeStruct((depth, ne // 2, d), jnp.uint32),
                   jax.ShapeDtypeStruct((depth, ne // ec, d // 2, ec), jnp.uint32)],
        compiler_params=_cparams("arbitrary", "arbitrary"),
        name="expert_tables",
    )(peer_u, peer_v)


def _peer_call(x, h2, mod, c1, e1, r2, e2, u, vt, layer, final_g, seq, final):
    t, d = x.shape
    _, nk, _, ec = vt.shape
    tm = PEER_TM
    n_i = t // tm
    last = n_i * nk - 1
    cur_i = lambda s: jnp.minimum(s, last) // nk
    cur_k = lambda s: jnp.minimum(s, last) % nk
    prev_i = lambda s: jnp.maximum(s - 1, 0) // nk
    prev_k = lambda s: jnp.maximum(s - 1, 0) % nk
    once = pl.Buffered(1)
    tab = lambda: pl.BlockSpec((PEER_HEADS, NKEYS // 2, tm), lambda s: (0, 0, prev_i(s)))
    rows = lambda: pl.BlockSpec((PEER_HEADS, nk, NKEYS // nk, tm), lambda s: (0, 0, 0, prev_i(s)),
                                pipeline_mode=once)
    c1 = c1.reshape(PEER_HEADS, nk, NKEYS // nk, t)
    e1 = e1.reshape(PEER_HEADS, nk, NKEYS // nk, t)
    return pl.pallas_call(
        functools.partial(_peer_kernel, nk=nk, final=final),
        grid=(n_i * nk + 1,),
        in_specs=[pl.BlockSpec((d // 2, tm), lambda s: (0, cur_i(s))),
                  pl.BlockSpec((None, ec // 2, d), lambda s: (layer, cur_k(s), 0)),
                  pl.BlockSpec((tm, d), lambda s: (prev_i(s), 0)),
                  pl.BlockSpec((1, 1, N_MOD * d), lambda s: ((prev_i(s) * tm) // seq, 0, 0)),
                  rows(), rows(), tab(), tab(),
                  pl.BlockSpec((None, None, d // 2, ec), lambda s: (layer, prev_k(s), 0, 0)),
                  pl.BlockSpec((1, d), lambda s: (0, 0))],
        out_specs=pl.BlockSpec((tm, d), lambda s: (prev_i(s), 0), pipeline_mode=once),
        out_shape=jax.ShapeDtypeStruct((t, d), F32),
        scratch_shapes=[pltpu.VMEM((d, tm), F32),
                        pltpu.VMEM((ec, tm), BF16),
                        pltpu.VMEM((ec, tm), BF16)],
        compiler_params=_cparams("arbitrary"),
        name="peer_dense",
    )(h2, u, x, mod, c1, e1, r2, e2, vt, final_g)


def kernel(x, c, rel_bias, w_mod, b_mod, norm1_g, norm2_g, w_in, w_out, diff_lambda, subln_g, conf_dw, conf_ln_g, conf_ln_b, sconv_w, sgu_ln_g, sgu_ln_b, sgu_w, sgu_b, peer_wq, peer_keys, peer_u, peer_v, final_g):
    batch, seq, d = x.shape
    depth = w_in.shape[0]
    t = batch * seq
    assert w_in.shape[2] == N_SLICES * GROUP and seq % PEER_TM == 0 and d % LANES == 0

    xf = x.reshape(t, d)
    mods = _mod_call(c, w_mod, b_mod).reshape(depth, batch, 1, N_MOD * d)
    attn_bias = _attn_bias_tiles(rel_bias)
    u_words, vt_words = _expert_tables_call(peer_u, peer_v)
    row = lambda v: v.reshape(1, -1)

    for l in range(depth):
        lam_init = 0.8 - 0.6 * math.exp(-0.3 * l)
        wl = w_in[l]
        wqv_t = jnp.transpose(jnp.concatenate([wl[:, 0:GROUP], wl[:, 2 * GROUP:3 * GROUP]], axis=1)).astype(BF16)
        w_rest = jnp.concatenate([wl[:, GROUP:2 * GROUP], wl[:, 3 * GROUP:]], axis=1).astype(BF16)
        qt, k, vt, pc = _in_call(xf, mods[l], row(norm1_g[l]), wqv_t, w_rest, seq)
        yat = _attn_call(qt, k, vt, attn_bias, diff_lambda[l], subln_g[l].reshape(ATT_V, 1),
                         batch, seq, lam_init)
        sgu_bias_tile = jnp.repeat(jnp.transpose(sgu_b[l]), GROUP // SGU_HEADS, axis=1)
        ybcd = _conv_call(pc, conf_dw[l], row(conf_ln_g[l]), row(conf_ln_b[l]), sconv_w[l],
                          row(sgu_ln_g[l]), row(sgu_ln_b[l]), sgu_w[l], sgu_bias_tile, batch, seq)
        keys = peer_keys[l].reshape(2 * PEER_HEADS, NKEYS, -1).astype(BF16)
        xf, h2, c1, e1, r2, e2 = _outtop_call(xf, yat, ybcd, mods[l], w_out[l].astype(BF16),
                                              row(norm2_g[l]), peer_wq[l].astype(BF16), keys, seq)
        xf = _peer_call(xf, h2, mods[l], c1, e1, r2, e2, u_words, vt_words, l, row(final_g),
                        seq, l == depth - 1)
    return xf.reshape(batch, seq, d)
```

```python
import functools
import math

import jax
import jax.numpy as jnp
import numpy as np
from jax import lax
from jax.experimental import pallas as pl
from jax.experimental.pallas import tpu as pltpu

F32 = jnp.float32
BF16 = jnp.bfloat16
I32 = jnp.int32
HIGHEST = lax.Precision.HIGHEST

GROUP = 256
N_SLICES = 10
ATT_HEADS = 4
ATT_V = 64
ATT_QK = 32
N_MAPS = 2 * ATT_HEADS
NUM_BUCKETS = 32
MAX_DISTANCE = 128
CONF_W = 31
SCONV_W = 3
SGU_HEADS = 4
SGU_CHUNK = 128
PEER_HEADS = 8
NKEYS = 128
TOPK = 16
N_MOD = 6
EPS = 1e-6
NEG_INF = -1e30
SQRT_HALF = float(np.sqrt(0.5))

LANES = 128
SUBLANES = 8
VMEM_LIMIT_BYTES = 56 * 1024 * 1024

PEER_TM = 1024
PEER_EC = 1024
PEER_TN = 256
TOPK_TM = 512
TOPK_TW = 256
HEADS_PER_ITER = 2
ATQ = 256
ONES_ROWS = 16

CAND_COUNTS = [TOPK // (a + 1) for a in range(TOPK)]
CAND_STARTS = [int(v) for v in np.cumsum([0] + CAND_COUNTS[:-1])]
N_CAND = sum(CAND_COUNTS)
N_CAND_PAD = -(-N_CAND // SUBLANES) * SUBLANES
CAND_A = [a for a in range(TOPK) for _ in range(CAND_COUNTS[a])] + [None] * (N_CAND_PAD - N_CAND)
CAND_B = [b for a in range(TOPK) for b in range(CAND_COUNTS[a])] + [None] * (N_CAND_PAD - N_CAND)

_NT = (((1,), (1,)), ((), ()))
_TN = (((0,), (0,)), ((), ()))


def _as_words(x):
    return pltpu.bitcast(x, jnp.uint32)


def _as_bf16(words):
    return pltpu.bitcast(words, BF16)


def _gelu(x):
    return 0.5 * x * (1.0 + lax.erf(x * SQRT_HALF))


def _cparams(*sem):
    return pltpu.CompilerParams(dimension_semantics=sem, vmem_limit_bytes=VMEM_LIMIT_BYTES)


def _mod_kernel(c_ref, w_ref, b_ref, o_ref):
    c = c_ref[...]
    cond = c * jax.nn.sigmoid(c)
    o_ref[0] = jnp.dot(cond.astype(BF16), w_ref[0].astype(BF16), preferred_element_type=F32) + b_ref[0]


def _mod_call(c, w_mod, b_mod):
    depth, d, n = w_mod.shape
    b = c.shape[0]
    tn = 2048
    return pl.pallas_call(
        _mod_kernel,
        grid=(depth, n // tn),
        in_specs=[pl.BlockSpec((b, d), lambda l, j: (0, 0)),
                  pl.BlockSpec((1, d, tn), lambda l, j: (l, 0, j)),
                  pl.BlockSpec((1, 1, tn), lambda l, j: (l, 0, j))],
        out_specs=pl.BlockSpec((1, b, tn), lambda l, j: (l, 0, j)),
        out_shape=jax.ShapeDtypeStruct((depth, b, n), F32),
        compiler_params=_cparams("arbitrary", "arbitrary"),
        name="mod",
    )(c, w_mod, b_mod.reshape(depth, 1, n))


def _modulated_norm(x, g, shift, scale):
    ms = jnp.mean(x * x, axis=-1, keepdims=True)
    return (x * lax.rsqrt(ms + EPS) * g) * (1.0 + scale) + shift


def _in_kernel(x_ref, mod_ref, g_ref, wqv_ref, wr_ref, qt_ref, k_ref, vt_ref, pc_ref):
    d = x_ref.shape[1]
    h = _modulated_norm(x_ref[...], g_ref[...], mod_ref[0, :, 0:d], mod_ref[0, :, d:2 * d]).astype(BF16)
    qv = lax.dot_general(wqv_ref[...], h, _NT, preferred_element_type=F32)
    qt_ref[...] = (qv[0:GROUP] * (ATT_QK ** -0.5)).astype(BF16)
    for c in range(vt_ref.shape[0]):
        vt_ref[c] = qv[GROUP:2 * GROUP, c * ATQ:(c + 1) * ATQ].astype(BF16)
    res = jnp.dot(h, wr_ref[...], preferred_element_type=F32)
    k_ref[...] = res[:, 0:GROUP].astype(BF16)
    pc_ref[...] = res[:, GROUP:]


def _in_call(x, mod, g, wqv_t, w_rest, seq):
    t, d = x.shape
    n = w_rest.shape[1]
    tm = 512
    return pl.pallas_call(
        _in_kernel,
        grid=(t // tm,),
        in_specs=[pl.BlockSpec((tm, d), lambda i: (i, 0)),
                  pl.BlockSpec((1, 1, N_MOD * d), lambda i: ((i * tm) // seq, 0, 0)),
                  pl.BlockSpec((1, d), lambda i: (0, 0)),
                  pl.BlockSpec((2 * GROUP, d), lambda i: (0, 0)),
                  pl.BlockSpec((d, n), lambda i: (0, 0))],
        out_specs=[pl.BlockSpec((GROUP, tm), lambda i: (0, i)),
                   pl.BlockSpec((tm, GROUP), lambda i: (i, 0)),
                   pl.BlockSpec((tm // ATQ, GROUP, ATQ), lambda i: (i, 0, 0)),
                   pl.BlockSpec((tm, n - GROUP), lambda i: (i, 0))],
        out_shape=[jax.ShapeDtypeStruct((GROUP, t), BF16),
                   jax.ShapeDtypeStruct((t, GROUP), BF16),
                   jax.ShapeDtypeStruct((t // ATQ, GROUP, ATQ), BF16),
                   jax.ShapeDtypeStruct((t, n - GROUP), F32)],
        compiler_params=_cparams("arbitrary"),
        name="in_proj",
    )(x, mod, g, wqv_t, w_rest)


def _attn_kernel(qt_ref, k_ref, vt_ref, bias_ref, lam_ref, g_ref, o_ref, qz_ref, acc_ref, m_ref, *, lam_init):
    i = pl.program_id(1)
    qt = qt_ref[...]
    rowi = lax.broadcasted_iota(I32, (GROUP, ATQ), 0)
    for c in range(N_MAPS):
        lo = c * ATT_QK
        qz_ref[c] = jnp.where((rowi >= lo) & (rowi < lo + ATT_QK), qt, jnp.zeros_like(qt))
    ones = jnp.ones((ONES_ROWS, ATQ), BF16)

    def step(jb, bias_idx, init):
        kb = k_ref[pl.ds(pl.multiple_of(jb * ATQ, ATQ), ATQ), :]
        vb = vt_ref[jb]
        m_old = None if init else m_ref[...]
        acc_old = None if init else [acc_ref[c] for c in range(N_MAPS)]
        scores = [jnp.dot(kb, qz_ref[c], preferred_element_type=F32) for c in range(N_MAPS)]
        m_new, probs, alphas = [], [], []
        for c in range(N_MAPS):
            s = scores[c]
            if bias_idx is not None:
                s = s + bias_ref[c // 2, bias_idx]
            mn = jnp.max(s, axis=0, keepdims=True)
            if not init:
                mo = m_old[c:c + 1, :]
                mn = jnp.maximum(mo, mn)
                alphas.append(jnp.exp(mo - mn))
            probs.append(jnp.exp(s - mn).astype(BF16))
            m_new.append(mn)
        acc_new = []
        for c in range(N_MAPS):
            h = c // 2
            vaug = jnp.concatenate([vb[h * ATT_V:(h + 1) * ATT_V, :], ones], axis=0)
            pv = jnp.dot(vaug, probs[c], preferred_element_type=F32)
            acc_new.append(pv if init else alphas[c] * acc_old[c] + pv)
        m_ref[...] = jnp.concatenate(m_new, axis=0)
        for c in range(N_MAPS):
            acc_ref[c] = acc_new[c]

    step(i, 1, True)

    @pl.when(i >= 1)
    def _():
        step(i - 1, 0, False)

    def far(j, carry):
        step(j, None, False)
        return carry

    lax.fori_loop(0, jnp.maximum(i - 1, 0), far, 0)

    lp = lam_ref[...]
    lam = (jnp.exp(jnp.sum(lp[0:1] * lp[1:2], axis=1, keepdims=True))
           - jnp.exp(jnp.sum(lp[2:3] * lp[3:4], axis=1, keepdims=True)) + lam_init)
    for h in range(ATT_HEADS):
        a0 = acc_ref[2 * h]
        a1 = acc_ref[2 * h + 1]
        o = a0[0:ATT_V] / a0[ATT_V:ATT_V + 1] - lam * (a1[0:ATT_V] / a1[ATT_V:ATT_V + 1])
        ms = jnp.mean(o * o, axis=0, keepdims=True)
        y = o * lax.rsqrt(ms + EPS) * g_ref[...] * (1.0 - lam_init)
        o_ref[h * ATT_V:(h + 1) * ATT_V, :] = y.astype(o_ref.dtype)


def _attn_call(qt, k, vt, bias, lam_par, subln_col, batch, seq, lam_init):
    t = k.shape[0]
    nq = seq // ATQ
    return pl.pallas_call(
        functools.partial(_attn_kernel, lam_init=lam_init),
        grid=(batch, nq),
        in_specs=[pl.BlockSpec((GROUP, ATQ), lambda b, i: (0, b * nq + i)),
                  pl.BlockSpec((seq, GROUP), lambda b, i: (b, 0)),
                  pl.BlockSpec((nq, GROUP, ATQ), lambda b, i: (b, 0, 0)),
                  pl.BlockSpec((ATT_HEADS, 2, ATQ, ATQ), lambda b, i: (0, 0, 0, 0)),
                  pl.BlockSpec((4, ATT_QK), lambda b, i: (0, 0)),
                  pl.BlockSpec((ATT_V, 1), lambda b, i: (0, 0))],
        out_specs=pl.BlockSpec((GROUP, ATQ), lambda b, i: (0, b * nq + i)),
        out_shape=jax.ShapeDtypeStruct((GROUP, t), BF16),
        scratch_shapes=[pltpu.VMEM((N_MAPS, GROUP, ATQ), BF16),
                        pltpu.VMEM((N_MAPS, ATT_V + ONES_ROWS, ATQ), F32),
                        pltpu.VMEM((N_MAPS, ATQ), F32)],
        compiler_params=_cparams("arbitrary", "arbitrary"),
        name="diff_attn",
    )(qt, k, vt, bias, lam_par, subln_col)


def _rel_bucket_np(n):
    max_exact = NUM_BUCKETS // 2
    n = np.maximum(n, 0)
    ratio = (np.log(np.maximum(n, 1).astype(np.float32) / np.float32(max_exact))
             / np.float32(math.log(MAX_DISTANCE / max_exact)))
    large = np.minimum(max_exact + (ratio * (NUM_BUCKETS - max_exact)).astype(np.int32), NUM_BUCKETS - 1)
    return np.where(n < max_exact, n, large).astype(np.int32)


def _bias_kernel(bucket_ref, rb_ref, o_ref):
    h = pl.program_id(0)
    far = rb_ref[NUM_BUCKETS - 1, h]
    for tile in range(2):
        b = bucket_ref[tile]
        acc = jnp.full(b.shape, NEG_INF, F32)
        for k in range(NUM_BUCKETS):
            acc = jnp.where(b == k, rb_ref[k, h] - far, acc)
        o_ref[0, tile] = acc


def _attn_bias_tiles(rel_bias):
    rel_cur = np.arange(ATQ)[None, :] - np.arange(ATQ)[:, None]
    assert np.all(_rel_bucket_np(np.arange(ATQ + 1, 8 * ATQ)) == NUM_BUCKETS - 1)
    buckets = np.stack([_rel_bucket_np(rel_cur + ATQ),
                        np.where(rel_cur >= 0, _rel_bucket_np(rel_cur), NUM_BUCKETS)]).astype(np.int32)
    return pl.pallas_call(
        _bias_kernel,
        grid=(ATT_HEADS,),
        in_specs=[pl.BlockSpec((2, ATQ, ATQ), lambda h: (0, 0, 0)),
                  pl.BlockSpec(memory_space=pltpu.SMEM)],
        out_specs=pl.BlockSpec((1, 2, ATQ, ATQ), lambda h: (h, 0, 0, 0)),
        out_shape=jax.ShapeDtypeStruct((ATT_HEADS, 2, ATQ, ATQ), F32),
        compiler_params=_cparams("arbitrary"),
        name="attn_bias",
    )(jnp.asarray(buckets), rel_bias)


CONV_HALO = 32
SCONV_HALO = 8


def _layer_norm(x, g, b):
    mu = jnp.mean(x, axis=-1, keepdims=True)
    xc = x - mu
    var = jnp.mean(xc * xc, axis=-1, keepdims=True)
    return xc * lax.rsqrt(var + EPS) * g + b


def _conv_kernel(cur_ref, hga_ref, hgb_ref, hcc_ref, hch_ref, dw_ref, lng_ref, lnb_ref, sw_ref,
                 sg_ref, sb_ref, ws_ref, bs_ref, o_ref, ext_ref, ext2_ref, shift_ref):
    i = pl.program_id(1)
    ts = cur_ref.shape[0]
    first = i == 0

    def col(k):
        return cur_ref[:, k * GROUP:(k + 1) * GROUP]

    zh = hga_ref[...] * jax.nn.sigmoid(hgb_ref[...])
    ext_ref[0:CONV_HALO, :] = jnp.where(first, 0.0, zh)
    ext_ref[CONV_HALO:, :] = col(0) * jax.nn.sigmoid(col(1))
    acc = jnp.zeros((ts, GROUP), F32)
    first_off = CONV_HALO - (CONF_W - 1)
    for r in range(SUBLANES):
        taps = [j for j in range(CONF_W) if (first_off + j) % SUBLANES == r]
        if r == 0:
            src = ext_ref
        else:
            shift_ref[...] = ext_ref[r:r + shift_ref.shape[0], :]
            src = shift_ref
        for j in taps:
            q = first_off + j - r
            acc = acc + dw_ref[j:j + 1, :] * src[q:q + ts, :]
    zb = _layer_norm(acc, lng_ref[...], lnb_ref[...])
    o_ref[:, 0:GROUP] = (zb * jax.nn.sigmoid(zb)).astype(o_ref.dtype)

    ext2_ref[0:SCONV_HALO, :] = jnp.where(first, 0.0, hcc_ref[...] * hch_ref[...])
    ext2_ref[SCONV_HALO:, :] = col(3) * col(4)
    acc = jnp.zeros((ts, GROUP), F32)
    for j in range(SCONV_W):
        off = SCONV_HALO - (SCONV_W - 1) + j
        acc = acc + sw_ref[j:j + 1, :] * ext2_ref[off:off + ts, :]
    o_ref[:, GROUP:2 * GROUP] = (col(2) * acc).astype(o_ref.dtype)

    lane = lax.broadcasted_iota(I32, (1, GROUP), 1)
    rr = lax.broadcasted_iota(I32, (SGU_CHUNK, SGU_CHUNK), 0)
    cc = lax.broadcasted_iota(I32, (SGU_CHUNK, SGU_CHUNK), 1)
    hd = GROUP // SGU_HEADS
    w_tril = [jnp.where(rr >= cc, ws_ref[h], 0.0).astype(BF16) for h in range(SGU_HEADS)]
    for c in range(ts // SGU_CHUNK):
        rows = slice(c * SGU_CHUNK, (c + 1) * SGU_CHUNK)
        u = _gelu(cur_ref[rows, 5 * GROUP:6 * GROUP])
        v = _layer_norm(_gelu(cur_ref[rows, 6 * GROUP:7 * GROUP]), sg_ref[...], sb_ref[...]).astype(BF16)
        z = bs_ref[...]
        for h in range(SGU_HEADS):
            zh_ = jnp.dot(w_tril[h], v, preferred_element_type=F32)
            z = z + jnp.where((lane >= h * hd) & (lane < (h + 1) * hd), zh_, 0.0)
        o_ref[rows, 2 * GROUP:3 * GROUP] = (u * z).astype(o_ref.dtype)


def _conv_call(pc, conf_dw, conf_ln_g, conf_ln_b, sconv_w, sgu_ln_g, sgu_ln_b, sgu_w, sgu_bias_tile,
               batch, seq):
    t, n = pc.shape
    ts = 512
    nst = seq // ts

    def halo_map(rows, colblk):
        per = ts // rows
        return lambda b, i: (jnp.maximum((b * nst + i) * per - 1, 0), colblk)

    vec = lambda: pl.BlockSpec((1, GROUP), lambda b, i: (0, 0))
    return pl.pallas_call(
        _conv_kernel,
        grid=(batch, nst),
        in_specs=[pl.BlockSpec((ts, n), lambda b, i: (b * nst + i, 0)),
                  pl.BlockSpec((CONV_HALO, GROUP), halo_map(CONV_HALO, 0)),
                  pl.BlockSpec((CONV_HALO, GROUP), halo_map(CONV_HALO, 1)),
                  pl.BlockSpec((SCONV_HALO, GROUP), halo_map(SCONV_HALO, 3)),
                  pl.BlockSpec((SCONV_HALO, GROUP), halo_map(SCONV_HALO, 4)),
                  pl.BlockSpec((CONF_W, GROUP), lambda b, i: (0, 0)),
                  vec(), vec(),
                  pl.BlockSpec((SCONV_W, GROUP), lambda b, i: (0, 0)),
                  vec(), vec(),
                  pl.BlockSpec((SGU_HEADS, SGU_CHUNK, SGU_CHUNK), lambda b, i: (0, 0, 0)),
                  pl.BlockSpec((SGU_CHUNK, GROUP), lambda b, i: (0, 0))],
        out_specs=pl.BlockSpec((ts, 3 * GROUP), lambda b, i: (b * nst + i, 0)),
        out_shape=jax.ShapeDtypeStruct((t, 3 * GROUP), BF16),
        scratch_shapes=[pltpu.VMEM((CONV_HALO + ts, GROUP), F32),
                        pltpu.VMEM((SCONV_HALO + ts, GROUP), F32),
                        pltpu.VMEM((CONV_HALO - SUBLANES + ts, GROUP), F32)],
        compiler_params=_cparams("arbitrary", "arbitrary"),
        name="conv_mix",
    )(pc, pc, pc, pc, pc, conf_dw, conf_ln_g, conf_ln_b, sconv_w, sgu_ln_g, sgu_ln_b, sgu_w,
      sgu_bias_tile)


def _extract_top(s, rowi, n_rounds, want_rank):
    sentinel = float(s.shape[0])
    vals = []
    mark = jnp.full(s.shape, float(n_rounds) if want_rank else 0.0, F32)
    for r in range(n_rounds):
        m = jnp.max(s, axis=0, keepdims=True)
        idx = jnp.min(jnp.where(s == m, rowi, sentinel), axis=0, keepdims=True)
        hit = rowi == idx
        mark = jnp.where(hit, float(r) if want_rank else 1.0, mark)
        s = jnp.where(hit, -jnp.inf, s)
        vals.append(m)
    return vals, mark


def _batcher_network(n):
    pairs = []

    def merge(lo, m, r):
        step = 2 * r
        if step < m:
            merge(lo, m, step)
            merge(lo + r, m, step)
            pairs.extend((i, i + r) for i in range(lo + r, lo + m - r, step))
        else:
            pairs.append((lo, lo + r))

    def sort(lo, m):
        if m > 1:
            sort(lo, m // 2)
            sort(lo + m // 2, m // 2)
            merge(lo, m, 1)

    sort(0, n)
    return pairs


def _compare_exchange(blocks, i, j):
    hi, lo = jnp.maximum(blocks[i], blocks[j]), jnp.minimum(blocks[i], blocks[j])
    blocks[i], blocks[j] = hi, lo


def _bitonic_merge(blocks):
    n = len(blocks)
    d = n // 2
    while d >= 1:
        for i in range(n):
            if i & d == 0:
                _compare_exchange(blocks, i, i + d)
        d //= 2
    return blocks


def _sorted_top(s):
    n = s.shape[0] // SUBLANES
    assert n == TOPK
    blocks = [s[j * SUBLANES:(j + 1) * SUBLANES, :] for j in range(n)]
    for i, j in _batcher_network(n):
        _compare_exchange(blocks, i, j)
    shift = SUBLANES // 2
    while shift >= 1:
        rolled = [pltpu.roll(b, shift, 0) for b in blocks]
        blocks = _bitonic_merge([jnp.maximum(blocks[j], rolled[n - 1 - j]) for j in range(n)])
        shift //= 2
    return blocks


def _extract_values(s, n_rounds):
    vals = []
    for _ in range(n_rounds):
        m = jnp.max(s, axis=0, keepdims=True)
        s = jnp.where(s == m, -jnp.inf, s)
        vals.append(m)
    return vals


def _tile_rows(block, n_rows):
    return jnp.concatenate([block] * (n_rows // block.shape[0]), axis=0)


def _pack_pair(x):
    bits = pltpu.bitcast(x.astype(BF16).astype(F32), jnp.uint32)
    return bits | (bits >> 16)


def _outtop_kernel(x_ref, yat_ref, yb_ref, mod_ref, wo_ref, g_ref, wq_ref, keys_ref,
                   xo_ref, h2t_ref, c1_ref, e1_ref, r2_ref, e2_ref, qs_ref):
    d = x_ref.shape[1]
    tm = x_ref.shape[0]
    y = lax.dot_general(yat_ref[...], wo_ref[0:GROUP, :], _TN, preferred_element_type=F32)
    y = y + jnp.dot(yb_ref[...], wo_ref[GROUP:, :], preferred_element_type=F32)
    x = x_ref[...] + mod_ref[0, :, 2 * d:3 * d] * y
    xo_ref[...] = x
    h2f = _modulated_norm(x, g_ref[...], mod_ref[0, :, 3 * d:4 * d], mod_ref[0, :, 4 * d:5 * d])
    h2 = h2f.astype(BF16)
    h2t_ref[...] = _as_words(jnp.transpose(h2f).astype(BF16))
    q = jnp.dot(h2, wq_ref[...], preferred_element_type=F32).astype(BF16)
    for hp in range(2 * PEER_HEADS):
        qs_ref[hp] = q[:, hp * NKEYS:(hp + 1) * NKEYS]

    tw = TOPK_TW
    rowi = lax.broadcasted_iota(I32, (NKEYS, tw), 0).astype(F32)
    rowc_f = lax.broadcasted_iota(I32, (N_CAND_PAD, tw), 0).astype(F32)
    sub = lax.broadcasted_iota(I32, (SUBLANES, tw), 0)

    def spread(vals, rank_of_row):
        pick = lambda r: jnp.full((1, tw), -jnp.inf, F32) if r is None else vals[r]
        block = jnp.broadcast_to(pick(rank_of_row[0]), (SUBLANES, tw))
        for p in range(1, SUBLANES):
            if rank_of_row[p] != rank_of_row[p - 1]:
                block = jnp.where(sub >= p, pick(rank_of_row[p]), block)
        return block

    def candidates(v1, v2):
        return jnp.concatenate(
            [spread(v1, CAND_A[k:k + SUBLANES]) + spread(v2, CAND_B[k:k + SUBLANES])
             for k in range(0, N_CAND_PAD, SUBLANES)], axis=0)

    def group_counts(sel):
        return [jnp.sum(sel[st:st + n], axis=0, keepdims=True) for st, n in zip(CAND_STARTS, CAND_COUNTS)]

    def store(hh, lanes, s1, s2, max1, max2, z, c1, r2):
        c1_ref[hh, :, lanes] = _pack_pair(c1)
        e1_ref[hh, :, lanes] = _pack_pair(jnp.exp(s1 - max1) / z)
        r2_ref[hh, :, lanes] = _as_words(r2.astype(BF16))
        e2_ref[hh, :, lanes] = _as_words(jnp.exp(s2 - max2).astype(BF16))

    def chain(hh, lanes):
        s1, s2 = [lax.dot_general(keys_ref[2 * hh + p], qs_ref[2 * hh + p, lanes, :], _NT,
                                  preferred_element_type=F32) for p in range(2)]

        v1 = _sorted_top(s1)
        v2 = _sorted_top(s2)
        cand = candidates(v1, v2)
        top = _extract_values(cand, TOPK)
        cmax, thr = top[0], top[TOPK - 1]
        z = jnp.zeros_like(cmax)
        for val in top:
            z = z + jnp.exp(val - cmax)
        counts = group_counts((cand >= thr).astype(F32))
        c1 = jnp.zeros((NKEYS, tw), F32)
        r2 = jnp.full((NKEYS, tw), float(TOPK), F32)
        for a in reversed(range(TOPK)):
            c1 = jnp.where(s1 == _tile_rows(v1[a], NKEYS), counts[a], c1)
            r2 = jnp.where(s2 == _tile_rows(v2[a], NKEYS), float(a), r2)
        store(hh, lanes, s1, s2, v1[0][0:1], v2[0][0:1], z, c1, r2)

        tied = jnp.sum(c1, axis=0, keepdims=True) != float(TOPK)
        for s, v in ((s1, v1), (s2, v2)):
            dup = v[0] == v[1]
            for a in range(1, TOPK - 1):
                dup = dup | (v[a] == v[a + 1])
            n_top = jnp.sum((s >= _tile_rows(v[TOPK - 1], NKEYS)).astype(F32), axis=0, keepdims=True)
            tied = tied | (n_top != float(TOPK)) | (jnp.max(dup.astype(F32), axis=0, keepdims=True) > 0.0)

        return s1, s2, jnp.max(tied.astype(F32)) > 0.0

    def exact_chain(hh, lanes, s1, s2, tied):
        @pl.when(tied)
        def _():
            w1, rank1 = _extract_top(s1, rowi, TOPK, True)
            w2, rank2 = _extract_top(s2, rowi, TOPK, True)
            cand_x = candidates(w1, w2)
            _, sel = _extract_top(cand_x, rowc_f, TOPK, False)
            cmax_x = w1[0] + w2[0]
            z_x = jnp.sum(sel * jnp.exp(jnp.where(sel > 0.0, cand_x, cmax_x) - cmax_x), axis=0, keepdims=True)
            counts_x = group_counts(sel)
            c1_x = jnp.zeros((NKEYS, tw), F32)
            for a in range(TOPK):
                c1_x = jnp.where(rank1 == float(a), counts_x[a], c1_x)
            store(hh, lanes, s1, s2, w1[0], w2[0], z_x, c1_x, rank2)

    def heads(i, carry):
        work = [(HEADS_PER_ITER * i + u, slice(c * tw, (c + 1) * tw))
                for u in range(HEADS_PER_ITER) for c in range(tm // tw)]
        fast = [chain(hh, lanes) for hh, lanes in work]
        for (hh, lanes), (s1, s2, tied) in zip(work, fast):
            exact_chain(hh, lanes, s1, s2, tied)
        return carry

    lax.fori_loop(0, PEER_HEADS // HEADS_PER_ITER, heads, 0)


def _outtop_call(x, yat, ybcd, mod, w_out, g2, wq, keys, seq):
    t, d = x.shape
    tm = TOPK_TM
    nq = wq.shape[1]
    tab = lambda: pl.BlockSpec((PEER_HEADS, NKEYS, tm), lambda i: (0, 0, i))
    half = lambda: pl.BlockSpec((PEER_HEADS, NKEYS // 2, tm), lambda i: (0, 0, i))
    pair_shape = jax.ShapeDtypeStruct((PEER_HEADS, NKEYS, t), jnp.uint32)
    half_shape = jax.ShapeDtypeStruct((PEER_HEADS, NKEYS // 2, t), jnp.uint32)
    return pl.pallas_call(
        _outtop_kernel,
        grid=(t // tm,),
        in_specs=[pl.BlockSpec((tm, d), lambda i: (i, 0)),
                  pl.BlockSpec((GROUP, tm), lambda i: (0, i)),
                  pl.BlockSpec((tm, 3 * GROUP), lambda i: (i, 0)),
                  pl.BlockSpec((1, 1, N_MOD * d), lambda i: ((i * tm) // seq, 0, 0)),
                  pl.BlockSpec((4 * GROUP, d), lambda i: (0, 0)),
                  pl.BlockSpec((1, d), lambda i: (0, 0)),
                  pl.BlockSpec((d, nq), lambda i: (0, 0)),
                  pl.BlockSpec((2 * PEER_HEADS, NKEYS, NKEYS), lambda i: (0, 0, 0))],
        out_specs=[pl.BlockSpec((tm, d), lambda i: (i, 0)),
                   pl.BlockSpec((d // 2, tm), lambda i: (0, i)),
                   tab(), tab(), half(), half()],
        out_shape=[jax.ShapeDtypeStruct((t, d), F32),
                   jax.ShapeDtypeStruct((d // 2, t), jnp.uint32),
                   pair_shape, pair_shape, half_shape, half_shape],
        scratch_shapes=[pltpu.VMEM((2 * PEER_HEADS, tm, NKEYS), BF16)],
        compiler_params=_cparams("arbitrary"),
        name="out_topk",
    )(x, yat, ybcd, mod, w_out, g2, wq, keys)


def _bcast_pair_rows(row_u32, n_rows):
    tile = pltpu.bitcast(jnp.broadcast_to(row_u32, (SUBLANES, row_u32.shape[1])), BF16)
    return jnp.concatenate([tile] * (n_rows // (2 * SUBLANES)), axis=0)


def _peer_kernel(h2t_ref, u_ref, x_ref, mod_ref, c1_ref, e1_ref, r2_ref, e2_ref, vt_ref, fg_ref,
                 o_ref, acc_ref, a0_ref, a1_ref, *, nk, final):
    s = pl.program_id(0)
    kp = jnp.maximum(s - 1, 0) % nk
    nb = vt_ref.shape[1] // NKEYS
    d = x_ref.shape[1]

    @pl.when(s == 0)
    def _():
        a1_ref[...] = jnp.zeros_like(a1_ref)

    @pl.when(kp == 0)
    def _():
        acc_ref[...] = jnp.zeros_like(acc_ref)

    n_col = x_ref.shape[0] // PEER_TN

    def body(a_cur, a_prev):
        def cols(j):
            return slice(j * PEER_TN, (j + 1) * PEER_TN)

        def first_matmul(j):
            a_cur[:, cols(j)] = jnp.dot(_as_bf16(u_ref[...]), _as_bf16(h2t_ref[:, cols(j)]),
                                        preferred_element_type=F32).astype(BF16)

        def gated_act(j):
            blocks = []
            for ii in range(nb):
                act = _gelu(a_prev[ii * NKEYS:(ii + 1) * NKEYS, cols(j)])
                gate = None
                for hh in range(PEER_HEADS):
                    c1 = _bcast_pair_rows(c1_ref[hh, kp, ii:ii + 1, cols(j)], NKEYS)
                    e1 = _bcast_pair_rows(e1_ref[hh, kp, ii:ii + 1, cols(j)], NKEYS)
                    g = jnp.where(_as_bf16(r2_ref[hh, :, cols(j)]) < c1, _as_bf16(e2_ref[hh, :, cols(j)]) * e1,
                                  jnp.zeros_like(e1))
                    gate = g if gate is None else gate + g
                blocks.append(gate * act)
            return jnp.concatenate(blocks, axis=0)

        def second_matmul(j, w):
            acc_ref[:, cols(j)] += jnp.dot(_as_bf16(vt_ref[...]), w, preferred_element_type=F32)

        first_matmul(0)
        for j in range(n_col):
            if j + 1 < n_col:
                first_matmul(j + 1)
            second_matmul(j, gated_act(j))

    @pl.when(s % 2 == 0)
    def _():
        body(a0_ref, a1_ref)

    @pl.when(s % 2 == 1)
    def _():
        body(a1_ref, a0_ref)

    @pl.when((kp == nk - 1) & (s >= 1))
    def _():
        y = jnp.transpose(acc_ref[...])
        x = x_ref[...] + mod_ref[0, :, 5 * d:6 * d] * y
        if final:
            ms = jnp.mean(x * x, axis=-1, keepdims=True)
            x = x * lax.rsqrt(ms + EPS) * fg_ref[...]
        o_ref[...] = x


def _expert_tables_kernel(u_ref, v_ref, uo_ref, vo_ref):
    uo_ref[...] = _as_words(u_ref[...].astype(BF16))
    vo_ref[...] = _as_words(jnp.transpose(v_ref[...]).astype(BF16))


def _expert_tables_call(peer_u, peer_v):
    depth, ne, d = peer_u.shape
    ec = PEER_EC
    return pl.pallas_call(
        _expert_tables_kernel,
        grid=(depth, ne // ec),
        in_specs=[pl.BlockSpec((None, ec, d), lambda l, k: (l, k, 0)),
                  pl.BlockSpec((None, ec, d), lambda l, k: (l, k, 0))],
        out_specs=[pl.BlockSpec((None, ec // 2, d), lambda l, k: (l, k, 0)),
                   pl.BlockSpec((None, None, d // 2, ec), lambda l, k: (l, k, 0, 0))],
        out_shape=[jax.ShapeDtypeStruct((depth, ne // 2, d), jnp.uint32),
                   jax.ShapeDtypeStruct((depth, ne // ec, d // 2, ec), jnp.uint32)],
        compiler_params=_cparams("arbitrary", "arbitrary"),
        name="expert_tables",
    )(peer_u, peer_v)


def _peer_call(x, h2, mod, c1, e1, r2, e2, u, vt, layer, final_g, seq, final):
    t, d = x.shape
    _, nk, _, ec = vt.shape
    tm = PEER_TM
    n_i = t // tm
    last = n_i * nk - 1
    cur_i = lambda s: jnp.minimum(s, last) // nk
    cur_k = lambda s: jnp.minimum(s, last) % nk
    prev_i = lambda s: jnp.maximum(s - 1, 0) // nk
    prev_k = lambda s: jnp.maximum(s - 1, 0) % nk
    once = pl.Buffered(1)
    tab = lambda: pl.BlockSpec((PEER_HEADS, NKEYS // 2, tm), lambda s: (0, 0, prev_i(s)))
    rows = lambda: pl.BlockSpec((PEER_HEADS, nk, NKEYS // nk, tm), lambda s: (0, 0, 0, prev_i(s)),
                                pipeline_mode=once)
    c1 = c1.reshape(PEER_HEADS, nk, NKEYS // nk, t)
    e1 = e1.reshape(PEER_HEADS, nk, NKEYS // nk, t)
    return pl.pallas_call(
        functools.partial(_peer_kernel, nk=nk, final=final),
        grid=(n_i * nk + 1,),
        in_specs=[pl.BlockSpec((d // 2, tm), lambda s: (0, cur_i(s))),
                  pl.BlockSpec((None, ec // 2, d), lambda s: (layer, cur_k(s), 0)),
                  pl.BlockSpec((tm, d), lambda s: (prev_i(s), 0)),
                  pl.BlockSpec((1, 1, N_MOD * d), lambda s: ((prev_i(s) * tm) // seq, 0, 0)),
                  rows(), rows(), tab(), tab(),
                  pl.BlockSpec((None, None, d // 2, ec), lambda s: (layer, prev_k(s), 0, 0)),
                  pl.BlockSpec((1, d), lambda s: (0, 0))],
        out_specs=pl.BlockSpec((tm, d), lambda s: (prev_i(s), 0), pipeline_mode=once),
        out_shape=jax.ShapeDtypeStruct((t, d), F32),
        scratch_shapes=[pltpu.VMEM((d, tm), F32),
                        pltpu.VMEM((ec, tm), BF16),
                        pltpu.VMEM((ec, tm), BF16)],
        compiler_params=_cparams("arbitrary"),
        name="peer_dense",
    )(h2, u, x, mod, c1, e1, r2, e2, vt, final_g)


def kernel(x, c, rel_bias, w_mod, b_mod, norm1_g, norm2_g, w_in, w_out, diff_lambda, subln_g, conf_dw, conf_ln_g, conf_ln_b, sconv_w, sgu_ln_g, sgu_ln_b, sgu_w, sgu_b, peer_wq, peer_keys, peer_u, peer_v, final_g):
    batch, seq, d = x.shape
    depth = w_in.shape[0]
    t = batch * seq
    assert w_in.shape[2] == N_SLICES * GROUP and seq % PEER_TM == 0 and d % LANES == 0

    xf = x.reshape(t, d)
    mods = _mod_call(c, w_mod, b_mod).reshape(depth, batch, 1, N_MOD * d)
    attn_bias = _attn_bias_tiles(rel_bias)
    u_words, vt_words = _expert_tables_call(peer_u, peer_v)
    row = lambda v: v.reshape(1, -1)

    for l in range(depth):
        lam_init = 0.8 - 0.6 * math.exp(-0.3 * l)
        wl = w_in[l]
        wqv_t = jnp.transpose(jnp.concatenate([wl[:, 0:GROUP], wl[:, 2 * GROUP:3 * GROUP]], axis=1)).astype(BF16)
        w_rest = jnp.concatenate([wl[:, GROUP:2 * GROUP], wl[:, 3 * GROUP:]], axis=1).astype(BF16)
        qt, k, vt, pc = _in_call(xf, mods[l], row(norm1_g[l]), wqv_t, w_rest, seq)
        yat = _attn_call(qt, k, vt, attn_bias, diff_lambda[l], subln_g[l].reshape(ATT_V, 1),
                         batch, seq, lam_init)
        sgu_bias_tile = jnp.repeat(jnp.transpose(sgu_b[l]), GROUP // SGU_HEADS, axis=1)
        ybcd = _conv_call(pc, conf_dw[l], row(conf_ln_g[l]), row(conf_ln_b[l]), sconv_w[l],
                          row(sgu_ln_g[l]), row(sgu_ln_b[l]), sgu_w[l], sgu_bias_tile, batch, seq)
        keys = peer_keys[l].reshape(2 * PEER_HEADS, NKEYS, -1).astype(BF16)
        xf, h2, c1, e1, r2, e2 = _outtop_call(xf, yat, ybcd, mods[l], w_out[l].astype(BF16),
                                              row(norm2_g[l]), peer_wq[l].astype(BF16), keys, seq)
        xf = _peer_call(xf, h2, mods[l], c1, e1, r2, e2, u_words, vt_words, l, row(final_g),
                        seq, l == depth - 1)
    return xf.reshape(batch, seq, d)
```

```python
import functools
import math

import jax
import jax.numpy as jnp
import numpy as np
from jax import lax
from jax.experimental import pallas as pl
from jax.experimental.pallas import tpu as pltpu

F32 = jnp.float32
BF16 = jnp.bfloat16
I32 = jnp.int32
HIGHEST = lax.Precision.HIGHEST

GROUP = 256
N_SLICES = 10
ATT_HEADS = 4
ATT_V = 64
ATT_QK = 32
N_MAPS = 2 * ATT_HEADS
NUM_BUCKETS = 32
MAX_DISTANCE = 128
CONF_W = 31
SCONV_W = 3
SGU_HEADS = 4
SGU_CHUNK = 128
PEER_HEADS = 8
NKEYS = 128
TOPK = 16
N_MOD = 6
EPS = 1e-6
NEG_INF = -1e30
SQRT_HALF = float(np.sqrt(0.5))

LANES = 128
SUBLANES = 8
VMEM_LIMIT_BYTES = 56 * 1024 * 1024

PEER_TM = 1024
PEER_EC = 1024
PEER_TN = 256
TOPK_TM = 512
TOPK_TW = 256
HEADS_PER_ITER = 2
ATQ = 256
ONES_ROWS = 16

CAND_COUNTS = [TOPK // (a + 1) for a in range(TOPK)]
CAND_STARTS = [int(v) for v in np.cumsum([0] + CAND_COUNTS[:-1])]
N_CAND = sum(CAND_COUNTS)
N_CAND_PAD = -(-N_CAND // SUBLANES) * SUBLANES
CAND_A = [a for a in range(TOPK) for _ in range(CAND_COUNTS[a])] + [None] * (N_CAND_PAD - N_CAND)
CAND_B = [b for a in range(TOPK) for b in range(CAND_COUNTS[a])] + [None] * (N_CAND_PAD - N_CAND)

_NT = (((1,), (1,)), ((), ()))
_TN = (((0,), (0,)), ((), ()))


def _as_words(x):
    return pltpu.bitcast(x, jnp.uint32)


def _as_bf16(words):
    return pltpu.bitcast(words, BF16)


def _gelu(x):
    return 0.5 * x * (1.0 + lax.erf(x * SQRT_HALF))


def _cparams(*sem):
    return pltpu.CompilerParams(dimension_semantics=sem, vmem_limit_bytes=VMEM_LIMIT_BYTES)


def _mod_kernel(c_ref, w_ref, b_ref, o_ref):
    c = c_ref[...]
    cond = c * jax.nn.sigmoid(c)
    o_ref[0] = jnp.dot(cond.astype(BF16), w_ref[0].astype(BF16), preferred_element_type=F32) + b_ref[0]


def _mod_call(c, w_mod, b_mod):
    depth, d, n = w_mod.shape
    b = c.shape[0]
    tn = 2048
    return pl.pallas_call(
        _mod_kernel,
        grid=(depth, n // tn),
        in_specs=[pl.BlockSpec((b, d), lambda l, j: (0, 0)),
                  pl.BlockSpec((1, d, tn), lambda l, j: (l, 0, j)),
                  pl.BlockSpec((1, 1, tn), lambda l, j: (l, 0, j))],
        out_specs=pl.BlockSpec((1, b, tn), lambda l, j: (l, 0, j)),
        out_shape=jax.ShapeDtypeStruct((depth, b, n), F32),
        compiler_params=_cparams("arbitrary", "arbitrary"),
        name="mod",
    )(c, w_mod, b_mod.reshape(depth, 1, n))


def _modulated_norm(x, g, shift, scale):
    ms = jnp.mean(x * x, axis=-1, keepdims=True)
    return (x * lax.rsqrt(ms + EPS) * g) * (1.0 + scale) + shift


def _in_kernel(x_ref, mod_ref, g_ref, wqv_ref, wr_ref, qt_ref, k_ref, vt_ref, pc_ref):
    d = x_ref.shape[1]
    h = _modulated_norm(x_ref[...], g_ref[...], mod_ref[0, :, 0:d], mod_ref[0, :, d:2 * d]).astype(BF16)
    qv = lax.dot_general(wqv_ref[...], h, _NT, preferred_element_type=F32)
    qt_ref[...] = (qv[0:GROUP] * (ATT_QK ** -0.5)).astype(BF16)
    for c in range(vt_ref.shape[0]):
        vt_ref[c] = qv[GROUP:2 * GROUP, c * ATQ:(c + 1) * ATQ].astype(BF16)
    res = jnp.dot(h, wr_ref[...], preferred_element_type=F32)
    k_ref[...] = res[:, 0:GROUP].astype(BF16)
    pc_ref[...] = res[:, GROUP:]


def _in_call(x, mod, g, wqv_t, w_rest, seq):
    t, d = x.shape
    n = w_rest.shape[1]
    tm = 512
    return pl.pallas_call(
        _in_kernel,
        grid=(t // tm,),
        in_specs=[pl.BlockSpec((tm, d), lambda i: (i, 0)),
                  pl.BlockSpec((1, 1, N_MOD * d), lambda i: ((i * tm) // seq, 0, 0)),
                  pl.BlockSpec((1, d), lambda i: (0, 0)),
                  pl.BlockSpec((2 * GROUP, d), lambda i: (0, 0)),
                  pl.BlockSpec((d, n), lambda i: (0, 0))],
        out_specs=[pl.BlockSpec((GROUP, tm), lambda i: (0, i)),
                   pl.BlockSpec((tm, GROUP), lambda i: (i, 0)),
                   pl.BlockSpec((tm // ATQ, GROUP, ATQ), lambda i: (i, 0, 0)),
                   pl.BlockSpec((tm, n - GROUP), lambda i: (i, 0))],
        out_shape=[jax.ShapeDtypeStruct((GROUP, t), BF16),
                   jax.ShapeDtypeStruct((t, GROUP), BF16),
                   jax.ShapeDtypeStruct((t // ATQ, GROUP, ATQ), BF16),
                   jax.ShapeDtypeStruct((t, n - GROUP), F32)],
        compiler_params=_cparams("arbitrary"),
        name="in_proj",
    )(x, mod, g, wqv_t, w_rest)


def _attn_kernel(qt_ref, k_ref, vt_ref, bias_ref, lam_ref, g_ref, o_ref, qz_ref, acc_ref, m_ref, *, lam_init):
    i = pl.program_id(1)
    qt = qt_ref[...]
    rowi = lax.broadcasted_iota(I32, (GROUP, ATQ), 0)
    for c in range(N_MAPS):
        lo = c * ATT_QK
        qz_ref[c] = jnp.where((rowi >= lo) & (rowi < lo + ATT_QK), qt, jnp.zeros_like(qt))
    ones = jnp.ones((ONES_ROWS, ATQ), BF16)

    def step(jb, bias_idx, init):
        kb = k_ref[pl.ds(pl.multiple_of(jb * ATQ, ATQ), ATQ), :]
        vb = vt_ref[jb]
        m_old = None if init else m_ref[...]
        acc_old = None if init else [acc_ref[c] for c in range(N_MAPS)]
        scores = [jnp.dot(kb, qz_ref[c], preferred_element_type=F32) for c in range(N_MAPS)]
        m_new, probs, alphas = [], [], []
        for c in range(N_MAPS):
            s = scores[c]
            if bias_idx is not None:
                s = s + bias_ref[c // 2, bias_idx]
            mn = jnp.max(s, axis=0, keepdims=True)
            if not init:
                mo = m_old[c:c + 1, :]
                mn = jnp.maximum(mo, mn)
                alphas.append(jnp.exp(mo - mn))
            probs.append(jnp.exp(s - mn).astype(BF16))
            m_new.append(mn)
        acc_new = []
        for c in range(N_MAPS):
            h = c // 2
            vaug = jnp.concatenate([vb[h * ATT_V:(h + 1) * ATT_V, :], ones], axis=0)
            pv = jnp.dot(vaug, probs[c], preferred_element_type=F32)
            acc_new.append(pv if init else alphas[c] * acc_old[c] + pv)
        m_ref[...] = jnp.concatenate(m_new, axis=0)
        for c in range(N_MAPS):
            acc_ref[c] = acc_new[c]

    step(i, 1, True)

    @pl.when(i >= 1)
    def _():
        step(i - 1, 0, False)

    n_far = jnp.maximum(i - 1, 0)

    def far_pair(j, carry):
        step(2 * j, None, False)
        step(2 * j + 1, None, False)
        return carry

    lax.fori_loop(0, n_far // 2, far_pair, 0)

    @pl.when(n_far % 2 == 1)
    def _():
        step(n_far - 1, None, False)

    lp = lam_ref[...]
    lam = (jnp.exp(jnp.sum(lp[0:1] * lp[1:2], axis=1, keepdims=True))
           - jnp.exp(jnp.sum(lp[2:3] * lp[3:4], axis=1, keepdims=True)) + lam_init)
    for h in range(ATT_HEADS):
        a0 = acc_ref[2 * h]
        a1 = acc_ref[2 * h + 1]
        o = a0[0:ATT_V] / a0[ATT_V:ATT_V + 1] - lam * (a1[0:ATT_V] / a1[ATT_V:ATT_V + 1])
        ms = jnp.mean(o * o, axis=0, keepdims=True)
        y = o * lax.rsqrt(ms + EPS) * g_ref[...] * (1.0 - lam_init)
        o_ref[h * ATT_V:(h + 1) * ATT_V, :] = y.astype(o_ref.dtype)


def _attn_call(qt, k, vt, bias, lam_par, subln_col, batch, seq, lam_init):
    t = k.shape[0]
    nq = seq // ATQ
    return pl.pallas_call(
        functools.partial(_attn_kernel, lam_init=lam_init),
        grid=(batch, nq),
        in_specs=[pl.BlockSpec((GROUP, ATQ), lambda b, i: (0, b * nq + i)),
                  pl.BlockSpec((seq, GROUP), lambda b, i: (b, 0)),
                  pl.BlockSpec((nq, GROUP, ATQ), lambda b, i: (b, 0, 0)),
                  pl.BlockSpec((ATT_HEADS, 2, ATQ, ATQ), lambda b, i: (0, 0, 0, 0)),
                  pl.BlockSpec((4, ATT_QK), lambda b, i: (0, 0)),
                  pl.BlockSpec((ATT_V, 1), lambda b, i: (0, 0))],
        out_specs=pl.BlockSpec((GROUP, ATQ), lambda b, i: (0, b * nq + i)),
        out_shape=jax.ShapeDtypeStruct((GROUP, t), BF16),
        scratch_shapes=[pltpu.VMEM((N_MAPS, GROUP, ATQ), BF16),
                        pltpu.VMEM((N_MAPS, ATT_V + ONES_ROWS, ATQ), F32),
                        pltpu.VMEM((N_MAPS, ATQ), F32)],
        compiler_params=_cparams("arbitrary", "arbitrary"),
        name="diff_attn",
    )(qt, k, vt, bias, lam_par, subln_col)


def _rel_bucket_np(n):
    max_exact = NUM_BUCKETS // 2
    n = np.maximum(n, 0)
    ratio = (np.log(np.maximum(n, 1).astype(np.float32) / np.float32(max_exact))
             / np.float32(math.log(MAX_DISTANCE / max_exact)))
    large = np.minimum(max_exact + (ratio * (NUM_BUCKETS - max_exact)).astype(np.int32), NUM_BUCKETS - 1)
    return np.where(n < max_exact, n, large).astype(np.int32)


def _bias_kernel(bucket_ref, rb_ref, o_ref):
    h = pl.program_id(0)
    far = rb_ref[NUM_BUCKETS - 1, h]
    for tile in range(2):
        b = bucket_ref[tile]
        acc = jnp.full(b.shape, NEG_INF, F32)
        for k in range(NUM_BUCKETS):
            acc = jnp.where(b == k, rb_ref[k, h] - far, acc)
        o_ref[0, tile] = acc


def _attn_bias_tiles(rel_bias):
    rel_cur = np.arange(ATQ)[None, :] - np.arange(ATQ)[:, None]
    assert np.all(_rel_bucket_np(np.arange(ATQ + 1, 8 * ATQ)) == NUM_BUCKETS - 1)
    buckets = np.stack([_rel_bucket_np(rel_cur + ATQ),
                        np.where(rel_cur >= 0, _rel_bucket_np(rel_cur), NUM_BUCKETS)]).astype(np.int32)
    return pl.pallas_call(
        _bias_kernel,
        grid=(ATT_HEADS,),
        in_specs=[pl.BlockSpec((2, ATQ, ATQ), lambda h: (0, 0, 0)),
                  pl.BlockSpec(memory_space=pltpu.SMEM)],
        out_specs=pl.BlockSpec((1, 2, ATQ, ATQ), lambda h: (h, 0, 0, 0)),
        out_shape=jax.ShapeDtypeStruct((ATT_HEADS, 2, ATQ, ATQ), F32),
        compiler_params=_cparams("arbitrary"),
        name="attn_bias",
    )(jnp.asarray(buckets), rel_bias)


CONV_HALO = 32
SCONV_HALO = 8


def _layer_norm(x, g, b):
    mu = jnp.mean(x, axis=-1, keepdims=True)
    xc = x - mu
    var = jnp.mean(xc * xc, axis=-1, keepdims=True)
    return xc * lax.rsqrt(var + EPS) * g + b


def _conv_kernel(cur_ref, hga_ref, hgb_ref, hcc_ref, hch_ref, dw_ref, lng_ref, lnb_ref, sw_ref,
                 sg_ref, sb_ref, ws_ref, bs_ref, o_ref, ext_ref, ext2_ref, shift_ref):
    i = pl.program_id(1)
    ts = cur_ref.shape[0]
    first = i == 0

    def col(k):
        return cur_ref[:, k * GROUP:(k + 1) * GROUP]

    zh = hga_ref[...] * jax.nn.sigmoid(hgb_ref[...])
    ext_ref[0:CONV_HALO, :] = jnp.where(first, 0.0, zh)
    ext_ref[CONV_HALO:, :] = col(0) * jax.nn.sigmoid(col(1))
    acc = jnp.zeros((ts, GROUP), F32)
    first_off = CONV_HALO - (CONF_W - 1)
    for r in range(SUBLANES):
        taps = [j for j in range(CONF_W) if (first_off + j) % SUBLANES == r]
        if r == 0:
            src = ext_ref
        else:
            shift_ref[...] = ext_ref[r:r + shift_ref.shape[0], :]
            src = shift_ref
        for j in taps:
            q = first_off + j - r
            acc = acc + dw_ref[j:j + 1, :] * src[q:q + ts, :]
    zb = _layer_norm(acc, lng_ref[...], lnb_ref[...])
    o_ref[:, 0:GROUP] = (zb * jax.nn.sigmoid(zb)).astype(o_ref.dtype)

    ext2_ref[0:SCONV_HALO, :] = jnp.where(first, 0.0, hcc_ref[...] * hch_ref[...])
    ext2_ref[SCONV_HALO:, :] = col(3) * col(4)
    acc = jnp.zeros((ts, GROUP), F32)
    for j in range(SCONV_W):
        off = SCONV_HALO - (SCONV_W - 1) + j
        acc = acc + sw_ref[j:j + 1, :] * ext2_ref[off:off + ts, :]
    o_ref[:, GROUP:2 * GROUP] = (col(2) * acc).astype(o_ref.dtype)

    lane = lax.broadcasted_iota(I32, (1, GROUP), 1)
    rr = lax.broadcasted_iota(I32, (SGU_CHUNK, SGU_CHUNK), 0)
    cc = lax.broadcasted_iota(I32, (SGU_CHUNK, SGU_CHUNK), 1)
    hd = GROUP // SGU_HEADS
    w_tril = [jnp.where(rr >= cc, ws_ref[h], 0.0).astype(BF16) for h in range(SGU_HEADS)]
    for c in range(ts // SGU_CHUNK):
        rows = slice(c * SGU_CHUNK, (c + 1) * SGU_CHUNK)
        u = _gelu(cur_ref[rows, 5 * GROUP:6 * GROUP])
        v = _layer_norm(_gelu(cur_ref[rows, 6 * GROUP:7 * GROUP]), sg_ref[...], sb_ref[...]).astype(BF16)
        z = bs_ref[...]
        for h in range(SGU_HEADS):
            zh_ = jnp.dot(w_tril[h], v, preferred_element_type=F32)
            z = z + jnp.where((lane >= h * hd) & (lane < (h + 1) * hd), zh_, 0.0)
        o_ref[rows, 2 * GROUP:3 * GROUP] = (u * z).astype(o_ref.dtype)


def _conv_call(pc, conf_dw, conf_ln_g, conf_ln_b, sconv_w, sgu_ln_g, sgu_ln_b, sgu_w, sgu_bias_tile,
               batch, seq):
    t, n = pc.shape
    ts = 512
    nst = seq // ts

    def halo_map(rows, colblk):
        per = ts // rows
        return lambda b, i: (jnp.maximum((b * nst + i) * per - 1, 0), colblk)

    vec = lambda: pl.BlockSpec((1, GROUP), lambda b, i: (0, 0))
    return pl.pallas_call(
        _conv_kernel,
        grid=(batch, nst),
        in_specs=[pl.BlockSpec((ts, n), lambda b, i: (b * nst + i, 0)),
                  pl.BlockSpec((CONV_HALO, GROUP), halo_map(CONV_HALO, 0)),
                  pl.BlockSpec((CONV_HALO, GROUP), halo_map(CONV_HALO, 1)),
                  pl.BlockSpec((SCONV_HALO, GROUP), halo_map(SCONV_HALO, 3)),
                  pl.BlockSpec((SCONV_HALO, GROUP), halo_map(SCONV_HALO, 4)),
                  pl.BlockSpec((CONF_W, GROUP), lambda b, i: (0, 0)),
                  vec(), vec(),
                  pl.BlockSpec((SCONV_W, GROUP), lambda b, i: (0, 0)),
                  vec(), vec(),
                  pl.BlockSpec((SGU_HEADS, SGU_CHUNK, SGU_CHUNK), lambda b, i: (0, 0, 0)),
                  pl.BlockSpec((SGU_CHUNK, GROUP), lambda b, i: (0, 0))],
        out_specs=pl.BlockSpec((ts, 3 * GROUP), lambda b, i: (b * nst + i, 0)),
        out_shape=jax.ShapeDtypeStruct((t, 3 * GROUP), BF16),
        scratch_shapes=[pltpu.VMEM((CONV_HALO + ts, GROUP), F32),
                        pltpu.VMEM((SCONV_HALO + ts, GROUP), F32),
                        pltpu.VMEM((CONV_HALO - SUBLANES + ts, GROUP), F32)],
        compiler_params=_cparams("arbitrary", "arbitrary"),
        name="conv_mix",
    )(pc, pc, pc, pc, pc, conf_dw, conf_ln_g, conf_ln_b, sconv_w, sgu_ln_g, sgu_ln_b, sgu_w,
      sgu_bias_tile)


def _extract_top(s, rowi, n_rounds, want_rank):
    sentinel = float(s.shape[0])
    vals = []
    mark = jnp.full(s.shape, float(n_rounds) if want_rank else 0.0, F32)
    for r in range(n_rounds):
        m = jnp.max(s, axis=0, keepdims=True)
        idx = jnp.min(jnp.where(s == m, rowi, sentinel), axis=0, keepdims=True)
        hit = rowi == idx
        mark = jnp.where(hit, float(r) if want_rank else 1.0, mark)
        s = jnp.where(hit, -jnp.inf, s)
        vals.append(m)
    return vals, mark


def _batcher_network(n):
    pairs = []

    def merge(lo, m, r):
        step = 2 * r
        if step < m:
            merge(lo, m, step)
            merge(lo + r, m, step)
            pairs.extend((i, i + r) for i in range(lo + r, lo + m - r, step))
        else:
            pairs.append((lo, lo + r))

    def sort(lo, m):
        if m > 1:
            sort(lo, m // 2)
            sort(lo + m // 2, m // 2)
            merge(lo, m, 1)

    sort(0, n)
    return pairs


def _compare_exchange(blocks, i, j):
    hi, lo = jnp.maximum(blocks[i], blocks[j]), jnp.minimum(blocks[i], blocks[j])
    blocks[i], blocks[j] = hi, lo


def _bitonic_merge(blocks):
    n = len(blocks)
    d = n // 2
    while d >= 1:
        for i in range(n):
            if i & d == 0:
                _compare_exchange(blocks, i, i + d)
        d //= 2
    return blocks


def _sorted_top(s):
    n = s.shape[0] // SUBLANES
    assert n == TOPK
    blocks = [s[j * SUBLANES:(j + 1) * SUBLANES, :] for j in range(n)]
    for i, j in _batcher_network(n):
        _compare_exchange(blocks, i, j)
    shift = SUBLANES // 2
    while shift >= 1:
        rolled = [pltpu.roll(b, shift, 0) for b in blocks]
        blocks = _bitonic_merge([jnp.maximum(blocks[j], rolled[n - 1 - j]) for j in range(n)])
        shift //= 2
    return blocks


def _extract_values(s, n_rounds):
    vals = []
    for _ in range(n_rounds):
        m = jnp.max(s, axis=0, keepdims=True)
        s = jnp.where(s == m, -jnp.inf, s)
        vals.append(m)
    return vals


def _tile_rows(block, n_rows):
    return jnp.concatenate([block] * (n_rows // block.shape[0]), axis=0)


def _pack_pair(x):
    bits = pltpu.bitcast(x.astype(BF16).astype(F32), jnp.uint32)
    return bits | (bits >> 16)


def _outtop_kernel(x_ref, yat_ref, yb_ref, mod_ref, wo_ref, g_ref, wq_ref, keys_ref,
                   xo_ref, h2t_ref, c1_ref, e1_ref, r2_ref, e2_ref, qs_ref):
    d = x_ref.shape[1]
    tm = x_ref.shape[0]
    y = lax.dot_general(yat_ref[...], wo_ref[0:GROUP, :], _TN, preferred_element_type=F32)
    y = y + jnp.dot(yb_ref[...], wo_ref[GROUP:, :], preferred_element_type=F32)
    x = x_ref[...] + mod_ref[0, :, 2 * d:3 * d] * y
    xo_ref[...] = x
    h2f = _modulated_norm(x, g_ref[...], mod_ref[0, :, 3 * d:4 * d], mod_ref[0, :, 4 * d:5 * d])
    h2 = h2f.astype(BF16)
    h2t_ref[...] = _as_words(jnp.transpose(h2f).astype(BF16))
    q = jnp.dot(h2, wq_ref[...], preferred_element_type=F32).astype(BF16)
    for hp in range(2 * PEER_HEADS):
        qs_ref[hp] = q[:, hp * NKEYS:(hp + 1) * NKEYS]

    tw = TOPK_TW
    rowi = lax.broadcasted_iota(I32, (NKEYS, tw), 0).astype(F32)
    rowc_f = lax.broadcasted_iota(I32, (N_CAND_PAD, tw), 0).astype(F32)
    sub = lax.broadcasted_iota(I32, (SUBLANES, tw), 0)

    def spread(vals, rank_of_row):
        pick = lambda r: jnp.full((1, tw), -jnp.inf, F32) if r is None else vals[r]
        block = jnp.broadcast_to(pick(rank_of_row[0]), (SUBLANES, tw))
        for p in range(1, SUBLANES):
            if rank_of_row[p] != rank_of_row[p - 1]:
                block = jnp.where(sub >= p, pick(rank_of_row[p]), block)
        return block

    def candidates(v1, v2):
        return jnp.concatenate(
            [spread(v1, CAND_A[k:k + SUBLANES]) + spread(v2, CAND_B[k:k + SUBLANES])
             for k in range(0, N_CAND_PAD, SUBLANES)], axis=0)

    def group_counts(sel):
        return [jnp.sum(sel[st:st + n], axis=0, keepdims=True) for st, n in zip(CAND_STARTS, CAND_COUNTS)]

    def store(hh, lanes, s1, s2, max1, max2, z, c1, r2):
        c1_ref[hh, :, lanes] = _pack_pair(c1)
        e1_ref[hh, :, lanes] = _pack_pair(jnp.exp(s1 - max1) / z)
        r2_ref[hh, :, lanes] = _as_words(r2.astype(BF16))
        e2_ref[hh, :, lanes] = _as_words(jnp.exp(s2 - max2).astype(BF16))

    def chain(hh, lanes):
        s1, s2 = [lax.dot_general(keys_ref[2 * hh + p], qs_ref[2 * hh + p, lanes, :], _NT,
                                  preferred_element_type=F32) for p in range(2)]

        v1 = _sorted_top(s1)
        v2 = _sorted_top(s2)
        cand = candidates(v1, v2)
        top = _extract_values(cand, TOPK)
        cmax, thr = top[0], top[TOPK - 1]
        z = jnp.zeros_like(cmax)
        for val in top:
            z = z + jnp.exp(val - cmax)
        counts = group_counts((cand >= thr).astype(F32))
        c1 = jnp.zeros((NKEYS, tw), F32)
        r2 = jnp.full((NKEYS, tw), float(TOPK), F32)
        for a in reversed(range(TOPK)):
            c1 = jnp.where(s1 == _tile_rows(v1[a], NKEYS), counts[a], c1)
            r2 = jnp.where(s2 == _tile_rows(v2[a], NKEYS), float(a), r2)
        store(hh, lanes, s1, s2, v1[0][0:1], v2[0][0:1], z, c1, r2)

        tied = jnp.sum(c1, axis=0, keepdims=True) != float(TOPK)
        for s, v in ((s1, v1), (s2, v2)):
            dup = v[0] == v[1]
            for a in range(1, TOPK - 1):
                dup = dup | (v[a] == v[a + 1])
            n_top = jnp.sum((s >= _tile_rows(v[TOPK - 1], NKEYS)).astype(F32), axis=0, keepdims=True)
            tied = tied | (n_top != float(TOPK)) | (jnp.max(dup.astype(F32), axis=0, keepdims=True) > 0.0)

        return s1, s2, jnp.max(tied.astype(F32)) > 0.0

    def exact_chain(hh, lanes, s1, s2, tied):
        @pl.when(tied)
        def _():
            w1, rank1 = _extract_top(s1, rowi, TOPK, True)
            w2, rank2 = _extract_top(s2, rowi, TOPK, True)
            cand_x = candidates(w1, w2)
            _, sel = _extract_top(cand_x, rowc_f, TOPK, False)
            cmax_x = w1[0] + w2[0]
            z_x = jnp.sum(sel * jnp.exp(jnp.where(sel > 0.0, cand_x, cmax_x) - cmax_x), axis=0, keepdims=True)
            counts_x = group_counts(sel)
            c1_x = jnp.zeros((NKEYS, tw), F32)
            for a in range(TOPK):
                c1_x = jnp.where(rank1 == float(a), counts_x[a], c1_x)
            store(hh, lanes, s1, s2, w1[0], w2[0], z_x, c1_x, rank2)

    def heads(i, carry):
        work = [(HEADS_PER_ITER * i + u, slice(c * tw, (c + 1) * tw))
                for u in range(HEADS_PER_ITER) for c in range(tm // tw)]
        fast = [chain(hh, lanes) for hh, lanes in work]
        for (hh, lanes), (s1, s2, tied) in zip(work, fast):
            exact_chain(hh, lanes, s1, s2, tied)
        return carry

    lax.fori_loop(0, PEER_HEADS // HEADS_PER_ITER, heads, 0)


def _outtop_call(x, yat, ybcd, mod, w_out, g2, wq, keys, seq):
    t, d = x.shape
    tm = TOPK_TM
    nq = wq.shape[1]
    tab = lambda: pl.BlockSpec((PEER_HEADS, NKEYS, tm), lambda i: (0, 0, i))
    half = lambda: pl.BlockSpec((PEER_HEADS, NKEYS // 2, tm), lambda i: (0, 0, i))
    pair_shape = jax.ShapeDtypeStruct((PEER_HEADS, NKEYS, t), jnp.uint32)
    half_shape = jax.ShapeDtypeStruct((PEER_HEADS, NKEYS // 2, t), jnp.uint32)
    return pl.pallas_call(
        _outtop_kernel,
        grid=(t // tm,),
        in_specs=[pl.BlockSpec((tm, d), lambda i: (i, 0)),
                  pl.BlockSpec((GROUP, tm), lambda i: (0, i)),
                  pl.BlockSpec((tm, 3 * GROUP), lambda i: (i, 0)),
                  pl.BlockSpec((1, 1, N_MOD * d), lambda i: ((i * tm) // seq, 0, 0)),
                  pl.BlockSpec((4 * GROUP, d), lambda i: (0, 0)),
                  pl.BlockSpec((1, d), lambda i: (0, 0)),
                  pl.BlockSpec((d, nq), lambda i: (0, 0)),
                  pl.BlockSpec((2 * PEER_HEADS, NKEYS, NKEYS), lambda i: (0, 0, 0))],
        out_specs=[pl.BlockSpec((tm, d), lambda i: (i, 0)),
                   pl.BlockSpec((d // 2, tm), lambda i: (0, i)),
                   tab(), tab(), half(), half()],
        out_shape=[jax.ShapeDtypeStruct((t, d), F32),
                   jax.ShapeDtypeStruct((d // 2, t), jnp.uint32),
                   pair_shape, pair_shape, half_shape, half_shape],
        scratch_shapes=[pltpu.VMEM((2 * PEER_HEADS, tm, NKEYS), BF16)],
        compiler_params=_cparams("arbitrary"),
        name="out_topk",
    )(x, yat, ybcd, mod, w_out, g2, wq, keys)


def _bcast_pair_rows(row_u32, n_rows):
    tile = pltpu.bitcast(jnp.broadcast_to(row_u32, (SUBLANES, row_u32.shape[1])), BF16)
    return jnp.concatenate([tile] * (n_rows // (2 * SUBLANES)), axis=0)


def _peer_kernel(h2t_ref, u_ref, x_ref, mod_ref, c1_ref, e1_ref, r2_ref, e2_ref, vt_ref, fg_ref,
                 o_ref, acc_ref, a0_ref, a1_ref, *, nk, final):
    s = pl.program_id(0)
    kp = jnp.maximum(s - 1, 0) % nk
    nb = vt_ref.shape[1] // NKEYS
    d = x_ref.shape[1]

    @pl.when(s == 0)
    def _():
        a1_ref[...] = jnp.zeros_like(a1_ref)

    @pl.when(kp == 0)
    def _():
        acc_ref[...] = jnp.zeros_like(acc_ref)

    n_col = x_ref.shape[0] // PEER_TN

    def body(a_cur, a_prev):
        def cols(j):
            return slice(j * PEER_TN, (j + 1) * PEER_TN)

        def first_matmul(j):
            a_cur[:, cols(j)] = jnp.dot(_as_bf16(u_ref[...]), _as_bf16(h2t_ref[:, cols(j)]),
                                        preferred_element_type=F32).astype(BF16)

        def gated_act(j):
            blocks = []
            for ii in range(nb):
                act = _gelu(a_prev[ii * NKEYS:(ii + 1) * NKEYS, cols(j)])
                gate = None
                for hh in range(PEER_HEADS):
                    c1 = _bcast_pair_rows(c1_ref[hh, kp, ii:ii + 1, cols(j)], NKEYS)
                    e1 = _bcast_pair_rows(e1_ref[hh, kp, ii:ii + 1, cols(j)], NKEYS)
                    g = jnp.where(_as_bf16(r2_ref[hh, :, cols(j)]) < c1, _as_bf16(e2_ref[hh, :, cols(j)]) * e1,
                                  jnp.zeros_like(e1))
                    gate = g if gate is None else gate + g
                blocks.append(gate * act)
            return jnp.concatenate(blocks, axis=0)

        def second_matmul(j, w):
            acc_ref[:, cols(j)] += jnp.dot(_as_bf16(vt_ref[...]), w, preferred_element_type=F32)

        first_matmul(0)
        for j in range(n_col):
            if j + 1 < n_col:
                first_matmul(j + 1)
            second_matmul(j, gated_act(j))

    @pl.when(s % 2 == 0)
    def _():
        body(a0_ref, a1_ref)

    @pl.when(s % 2 == 1)
    def _():
        body(a1_ref, a0_ref)

    @pl.when((kp == nk - 1) & (s >= 1))
    def _():
        y = jnp.transpose(acc_ref[...])
        x = x_ref[...] + mod_ref[0, :, 5 * d:6 * d] * y
        if final:
            ms = jnp.mean(x * x, axis=-1, keepdims=True)
            x = x * lax.rsqrt(ms + EPS) * fg_ref[...]
        o_ref[...] = x


def _expert_tables_kernel(u_ref, v_ref, uo_ref, vo_ref):
    uo_ref[...] = _as_words(u_ref[...].astype(BF16))
    vo_ref[...] = _as_words(jnp.transpose(v_ref[...]).astype(BF16))


def _expert_tables_call(peer_u, peer_v):
    depth, ne, d = peer_u.shape
    ec = PEER_EC
    return pl.pallas_call(
        _expert_tables_kernel,
        grid=(depth, ne // ec),
        in_specs=[pl.BlockSpec((None, ec, d), lambda l, k: (l, k, 0)),
                  pl.BlockSpec((None, ec, d), lambda l, k: (l, k, 0))],
        out_specs=[pl.BlockSpec((None, ec // 2, d), lambda l, k: (l, k, 0)),
                   pl.BlockSpec((None, None, d // 2, ec), lambda l, k: (l, k, 0, 0))],
        out_shape=[jax.ShapeDtypeStruct((depth, ne // 2, d), jnp.uint32),
                   jax.ShapeDtypeStruct((depth, ne // ec, d // 2, ec), jnp.uint32)],
        compiler_params=_cparams("arbitrary", "arbitrary"),
        name="expert_tables",
    )(peer_u, peer_v)


def _peer_call(x, h2, mod, c1, e1, r2, e2, u, vt, layer, final_g, seq, final):
    t, d = x.shape
    _, nk, _, ec = vt.shape
    tm = PEER_TM
    n_i = t // tm
    last = n_i * nk - 1
    cur_i = lambda s: jnp.minimum(s, last) // nk
    cur_k = lambda s: jnp.minimum(s, last) % nk
    prev_i = lambda s: jnp.maximum(s - 1, 0) // nk
    prev_k = lambda s: jnp.maximum(s - 1, 0) % nk
    once = pl.Buffered(1)
    tab = lambda: pl.BlockSpec((PEER_HEADS, NKEYS // 2, tm), lambda s: (0, 0, prev_i(s)))
    rows = lambda: pl.BlockSpec((PEER_HEADS, nk, NKEYS // nk, tm), lambda s: (0, 0, 0, prev_i(s)),
                                pipeline_mode=once)
    c1 = c1.reshape(PEER_HEADS, nk, NKEYS // nk, t)
    e1 = e1.reshape(PEER_HEADS, nk, NKEYS // nk, t)
    return pl.pallas_call(
        functools.partial(_peer_kernel, nk=nk, final=final),
        grid=(n_i * nk + 1,),
        in_specs=[pl.BlockSpec((d // 2, tm), lambda s: (0, cur_i(s))),
                  pl.BlockSpec((None, ec // 2, d), lambda s: (layer, cur_k(s), 0)),
                  pl.BlockSpec((tm, d), lambda s: (prev_i(s), 0)),
                  pl.BlockSpec((1, 1, N_MOD * d), lambda s: ((prev_i(s) * tm) // seq, 0, 0)),
                  rows(), rows(), tab(), tab(),
                  pl.BlockSpec((None, None, d // 2, ec), lambda s: (layer, prev_k(s), 0, 0)),
                  pl.BlockSpec((1, d), lambda s: (0, 0))],
        out_specs=pl.BlockSpec((tm, d), lambda s: (prev_i(s), 0), pipeline_mode=once),
        out_shape=jax.ShapeDtypeStruct((t, d), F32),
        scratch_shapes=[pltpu.VMEM((d, tm), F32),
                        pltpu.VMEM((ec, tm), BF16),
                        pltpu.VMEM((ec, tm), BF16)],
        compiler_params=_cparams("arbitrary"),
        name="peer_dense",
    )(h2, u, x, mod, c1, e1, r2, e2, vt, final_g)


def kernel(x, c, rel_bias, w_mod, b_mod, norm1_g, norm2_g, w_in, w_out, diff_lambda, subln_g, conf_dw, conf_ln_g, conf_ln_b, sconv_w, sgu_ln_g, sgu_ln_b, sgu_w, sgu_b, peer_wq, peer_keys, peer_u, peer_v, final_g):
    batch, seq, d = x.shape
    depth = w_in.shape[0]
    t = batch * seq
    assert w_in.shape[2] == N_SLICES * GROUP and seq % PEER_TM == 0 and d % LANES == 0

    xf = x.reshape(t, d)
    mods = _mod_call(c, w_mod, b_mod).reshape(depth, batch, 1, N_MOD * d)
    attn_bias = _attn_bias_tiles(rel_bias)
    u_words, vt_words = _expert_tables_call(peer_u, peer_v)
    row = lambda v: v.reshape(1, -1)

    for l in range(depth):
        lam_init = 0.8 - 0.6 * math.exp(-0.3 * l)
        wl = w_in[l]
        wqv_t = jnp.transpose(jnp.concatenate([wl[:, 0:GROUP], wl[:, 2 * GROUP:3 * GROUP]], axis=1)).astype(BF16)
        w_rest = jnp.concatenate([wl[:, GROUP:2 * GROUP], wl[:, 3 * GROUP:]], axis=1).astype(BF16)
        qt, k, vt, pc = _in_call(xf, mods[l], row(norm1_g[l]), wqv_t, w_rest, seq)
        yat = _attn_call(qt, k, vt, attn_bias, diff_lambda[l], subln_g[l].reshape(ATT_V, 1),
                         batch, seq, lam_init)
        sgu_bias_tile = jnp.repeat(jnp.transpose(sgu_b[l]), GROUP // SGU_HEADS, axis=1)
        ybcd = _conv_call(pc, conf_dw[l], row(conf_ln_g[l]), row(conf_ln_b[l]), sconv_w[l],
                          row(sgu_ln_g[l]), row(sgu_ln_b[l]), sgu_w[l], sgu_bias_tile, batch, seq)
        keys = peer_keys[l].reshape(2 * PEER_HEADS, NKEYS, -1).astype(BF16)
        xf, h2, c1, e1, r2, e2 = _outtop_call(xf, yat, ybcd, mods[l], w_out[l].astype(BF16),
                                              row(norm2_g[l]), peer_wq[l].astype(BF16), keys, seq)
        xf = _peer_call(xf, h2, mods[l], c1, e1, r2, e2, u_words, vt_words, l, row(final_g),
                        seq, l == depth - 1)
    return xf.reshape(batch, seq, d)
```
